```python
import jax, jax.numpy as jnp
from jax import lax
import numpy as np

D_MODEL = 1024
BATCH = 8
SEQ = 4096
DEPTH = 2

N_MIXERS = 2
N_CONV_LAYERS = (DEPTH + 1) // 2
N_ATTN_LAYERS = DEPTH // 2
CONV_WIDTH = 31
ATTN_HEADS = 16
HEAD_DIM = D_MODEL // ATTN_HEADS
Q_BLOCK = 128
PEER_HEADS = 8
N_KEYS = 128
N_EXPERTS = N_KEYS * N_KEYS
PEER_TOPK = 16
KEY_DIM = 256
HALF_KEY = KEY_DIM // 2
TOKEN_CHUNK = 128
LN_EPS = 1e-5
DN_ALPHA = (2 * DEPTH) ** 0.25
DN_BETA = (8 * DEPTH) ** -0.25

kernel_name = "hybrid_conv_fox_peer_deepnorm_adaln"


def _layernorm(x, g, b):
    x32 = x.astype(jnp.float32)
    mu = jnp.mean(x32, axis=-1, keepdims=True)
    var = jnp.mean(jnp.square(x32 - mu), axis=-1, keepdims=True)
    y = (x32 - mu) * lax.rsqrt(var + LN_EPS)
    return (y * g + b).astype(x.dtype)


def _adaln(c, w, b):
    mod = jax.nn.silu(c) @ w + b
    shift, scale, gate = jnp.split(mod, 3, axis=-1)
    return shift[:, None, :], scale[:, None, :], gate[:, None, :]


def _conv_module(h, w_in, b_in, w_dw, b_dw, ln_g, ln_b, w_out, b_out):
    a = jax.nn.glu(h @ w_in + b_in, axis=-1)
    a = lax.conv_general_dilated(
        a, w_dw[:, None, :],
        window_strides=(1,), padding=[(CONV_WIDTH - 1, 0)],
        dimension_numbers=("NWC", "WIO", "NWC"),
        feature_group_count=D_MODEL) + b_dw
    a = jax.nn.silu(_layernorm(a, ln_g, ln_b))
    return a @ w_out + b_out


def _fox_attention(h, w_in, b_in, w_out, b_out):
    B_, S_, _ = h.shape
    proj = h @ w_in + b_in
    q, k, v, f = jnp.split(proj, [D_MODEL, 2 * D_MODEL, 3 * D_MODEL], axis=-1)
    q = (q * HEAD_DIM ** -0.5).reshape(B_, S_, ATTN_HEADS, HEAD_DIM).transpose(0, 2, 1, 3)
    k = k.reshape(B_, S_, ATTN_HEADS, HEAD_DIM).transpose(0, 2, 1, 3)
    v = v.reshape(B_, S_, ATTN_HEADS, HEAD_DIM).transpose(0, 2, 1, 3)
    log_f = jax.nn.log_sigmoid(f.astype(jnp.float32))
    cum = lax.cumsum(log_f, axis=1).transpose(0, 2, 1)
    n_blk = S_ // Q_BLOCK
    q_blocks = q.reshape(B_, ATTN_HEADS, n_blk, Q_BLOCK, HEAD_DIM).transpose(2, 0, 1, 3, 4)
    cum_blocks = cum.reshape(B_, ATTN_HEADS, n_blk, Q_BLOCK).transpose(2, 0, 1, 3)
    key_pos = jnp.arange(S_)

    def one_block(args):
        qb, cb, blk = args
        logits = jnp.einsum("bhqd,bhkd->bhqk", qb, k, preferred_element_type=jnp.float32)
        logits = logits + cb[..., :, None] - cum[:, :, None, :]
        q_pos = blk * Q_BLOCK + jnp.arange(Q_BLOCK)
        causal = key_pos[None, :] <= q_pos[:, None]
        p = jax.nn.softmax(jnp.where(causal, logits, -jnp.inf), axis=-1)
        return jnp.einsum("bhqk,bhkd->bhqd", p.astype(v.dtype), v)

    out = lax.map(one_block, (q_blocks, cum_blocks, jnp.arange(n_blk)))
    out = out.transpose(1, 0, 3, 2, 4).reshape(B_, S_, D_MODEL)
    return out @ w_out + b_out


def _peer(h, w_query, sub_keys_1, sub_keys_2, expert_u, expert_v):
    B_, S_, _ = h.shape
    q = (h @ w_query).reshape(B_, S_, PEER_HEADS, 2, HALF_KEY)
    s1 = jnp.einsum("bshd,kd->bshk", q[..., 0, :], sub_keys_1, preferred_element_type=jnp.float32)
    s2 = jnp.einsum("bshd,kd->bshk", q[..., 1, :], sub_keys_2, preferred_element_type=jnp.float32)
    v1, i1 = lax.top_k(s1, PEER_TOPK)
    v2, i2 = lax.top_k(s2, PEER_TOPK)
    n_cand = PEER_TOPK * PEER_TOPK
    cand_s = (v1[..., :, None] + v2[..., None, :]).reshape(B_, S_, PEER_HEADS, n_cand)
    cand_i = (i1[..., :, None] * N_KEYS + i2[..., None, :]).reshape(B_, S_, PEER_HEADS, n_cand)
    top_s, pos = lax.top_k(cand_s, PEER_TOPK)
    experts = jnp.take_along_axis(cand_i, pos, axis=-1)
    gates = jax.nn.softmax(top_s, axis=-1)
    n_tok = B_ * S_
    n_chunk = n_tok // TOKEN_CHUNK
    n_sel = PEER_HEADS * PEER_TOPK
    xs = h.reshape(n_chunk, TOKEN_CHUNK, D_MODEL)
    es = experts.reshape(n_chunk, TOKEN_CHUNK, n_sel)
    gs = gates.reshape(n_chunk, TOKEN_CHUNK, n_sel).astype(h.dtype)

    def one_chunk(args):
        xc, ec, gc = args
        u = jnp.take(expert_u, ec, axis=0)
        vv = jnp.take(expert_v, ec, axis=0)
        act = jax.nn.gelu(jnp.einsum("tkd,td->tk", u, xc), approximate=False)
        return jnp.einsum("tk,tkd->td", gc * act, vv)

    out = lax.map(one_chunk, (xs, es, gs))
    return out.reshape(B_, S_, D_MODEL)


def setup_inputs(seed: int = 0) -> dict:
    key = jax.random.key(seed)
    ks = jax.random.split(key, 32)
    f32 = jnp.float32
    D = D_MODEL
    nrm = lambda k, shape, s: jax.random.normal(k, shape, f32) * s
    attn_cols = 3 * D + ATTN_HEADS
    attn_in_b = jnp.concatenate([
        nrm(ks[14], (N_ATTN_LAYERS, 3 * D), 0.02),
        jax.random.uniform(ks[15], (N_ATTN_LAYERS, ATTN_HEADS), f32, 1.0, 4.0)], axis=-1)
    return {
        "x": nrm(ks[0], (BATCH, SEQ, D), 1.0),
        "c": nrm(ks[1], (BATCH, D), 1.0),
        "ada_mix_w": nrm(ks[2], (DEPTH, D, 3 * D), 0.5 * D ** -0.5),
        "ada_mix_b": nrm(ks[3], (DEPTH, 3 * D), 0.02),
        "ln_mix_g": 1.0 + nrm(ks[4], (DEPTH, D), 0.02),
        "ln_mix_b": nrm(ks[5], (DEPTH, D), 0.02),
        "conv_in_w": nrm(ks[6], (N_CONV_LAYERS, D, 2 * D), D ** -0.5),
        "conv_in_b": nrm(ks[7], (N_CONV_LAYERS, 2 * D), 0.02),
        "conv_dw_w": nrm(ks[8], (N_CONV_LAYERS, CONV_WIDTH, D), CONV_WIDTH ** -0.5),
        "conv_dw_b": nrm(ks[9], (N_CONV_LAYERS, D), 0.02),
        "conv_ln_g": 1.0 + nrm(ks[10], (N_CONV_LAYERS, D), 0.02),
        "conv_ln_b": nrm(ks[11], (N_CONV_LAYERS, D), 0.02),
        "conv_out_w": nrm(ks[12], (N_CONV_LAYERS, D, D), DN_BETA * D ** -0.5),
        "conv_out_b": nrm(ks[13], (N_CONV_LAYERS, D), 0.02),
        "attn_in_w": nrm(ks[16], (N_ATTN_LAYERS, D, attn_cols), D ** -0.5),
        "attn_in_b": attn_in_b,
        "attn_out_w": nrm(ks[17], (N_ATTN_LAYERS, D, D), DN_BETA * D ** -0.5),
        "attn_out_b": nrm(ks[18], (N_ATTN_LAYERS, D), 0.02),
        "ada_ffn_w": nrm(ks[19], (DEPTH, D, 3 * D), 0.5 * D ** -0.5),
        "ada_ffn_b": nrm(ks[20], (DEPTH, 3 * D), 0.02),
        "ln_ffn_g": 1.0 + nrm(ks[21], (DEPTH, D), 0.02),
        "ln_ffn_b": nrm(ks[22], (DEPTH, D), 0.02),
        "peer_query_w": nrm(ks[23], (DEPTH, D, PEER_HEADS * KEY_DIM), D ** -0.5),
        "peer_sub_keys_1": nrm(ks[24], (DEPTH, N_KEYS, HALF_KEY), HALF_KEY ** -0.5),
        "peer_sub_keys_2": nrm(ks[25], (DEPTH, N_KEYS, HALF_KEY), HALF_KEY ** -0.5),
        "peer_expert_u": nrm(ks[26], (DEPTH, N_EXPERTS, D), D ** -0.5),
        "peer_expert_v": nrm(ks[27], (DEPTH, N_EXPERTS, D), DN_BETA * PEER_HEADS ** -0.5),
    }


def reference(x, c, ada_mix_w, ada_mix_b, ln_mix_g, ln_mix_b,
              conv_in_w, conv_in_b, conv_dw_w, conv_dw_b, conv_ln_g, conv_ln_b,
              conv_out_w, conv_out_b,
              attn_in_w, attn_in_b, attn_out_w, attn_out_b,
              ada_ffn_w, ada_ffn_b, ln_ffn_g, ln_ffn_b,
              peer_query_w, peer_sub_keys_1, peer_sub_keys_2, peer_expert_u, peer_expert_v):
    for i in range(DEPTH):
        j = i // N_MIXERS
        shift, scale, gate = _adaln(c, ada_mix_w[i], ada_mix_b[i])
        h = x * (1 + scale) + shift
        if i % N_MIXERS == 0:
            y = _conv_module(h, conv_in_w[j], conv_in_b[j], conv_dw_w[j], conv_dw_b[j],
                             conv_ln_g[j], conv_ln_b[j], conv_out_w[j], conv_out_b[j])
        else:
            y = _fox_attention(h, attn_in_w[j], attn_in_b[j], attn_out_w[j], attn_out_b[j])
        x = _layernorm(DN_ALPHA * x + gate * y, ln_mix_g[i], ln_mix_b[i])
        shift, scale, gate = _adaln(c, ada_ffn_w[i], ada_ffn_b[i])
        h = x * (1 + scale) + shift
        y = _peer(h, peer_query_w[i], peer_sub_keys_1[i], peer_sub_keys_2[i],
                  peer_expert_u[i], peer_expert_v[i])
        x = _layernorm(DN_ALPHA * x + gate * y, ln_ffn_g[i], ln_ffn_b[i])
    return x
```

```python
import functools
import math

import jax
import jax.numpy as jnp
from jax import lax
from jax.experimental import pallas as pl
from jax.experimental.pallas import tpu as pltpu

F32 = jnp.float32
BF16 = jnp.bfloat16

LN_EPS = 1e-5
CONV_HALO = 32
PEER_TOPK = 16
V7X_VMEM_LIMIT_BYTES = 56 * 1024 * 1024
SUBLANES = 8
LANES = 128
INV_SQRT2 = 1.0 / math.sqrt(2.0)
NEG_INF = float("-inf")


def _cparams(sem):
    return pltpu.CompilerParams(dimension_semantics=sem, vmem_limit_bytes=V7X_VMEM_LIMIT_BYTES)


def _layernorm(z, g, b):
    mu = jnp.mean(z, axis=-1, keepdims=True)
    zc = z - mu
    var = jnp.mean(zc * zc, axis=-1, keepdims=True)
    return zc * lax.rsqrt(var + LN_EPS) * g + b


def _split_mod(mod, d):
    return mod[:, :d], mod[:, d:2 * d], mod[:, 2 * d:]


def _ada_kernel(c_ref, w_ref, b_ref, o_ref):
    c = c_ref[...]
    sc = (c * jax.nn.sigmoid(c)).astype(BF16)
    o_ref[0] = jnp.dot(sc, w_ref[0].astype(BF16), preferred_element_type=F32) + b_ref[0]


def _adaln(c, w, b):
    n_layer, d, d3 = w.shape
    bn = c.shape[0]
    tn = min(d3, 1024)
    out = pl.pallas_call(
        _ada_kernel,
        grid=(n_layer, d3 // tn),
        in_specs=[pl.BlockSpec((bn, d), lambda l, j: (0, 0)),
                  pl.BlockSpec((1, d, tn), lambda l, j: (l, 0, j)),
                  pl.BlockSpec((1, 1, tn), lambda l, j: (l, 0, j))],
        out_specs=pl.BlockSpec((1, bn, tn), lambda l, j: (l, 0, j)),
        out_shape=jax.ShapeDtypeStruct((n_layer, bn, d3), F32),
        compiler_params=_cparams(("arbitrary", "arbitrary")),
        name="adaln",
    )(c, w, b.reshape(n_layer, 1, d3))
    return out.reshape(n_layer, bn, 1, d3)


def _conv_kernel(x_ref, mod_ref, w_in_ref, b_in_ref, w_dw_ref, b_dw_ref, cg_ref, cb_ref,
                 w_out_ref, b_out_ref, g_ref, b_ref, o_ref, abuf, *, width, alpha):
    t, d = x_ref.shape[1], x_ref.shape[2]

    @pl.when(pl.program_id(1) == 0)
    def _():
        abuf[0:CONV_HALO, :] = jnp.zeros((CONV_HALO, d), F32)

    x = x_ref[0]
    shift, scale, gate = _split_mod(mod_ref[0, 0], d)
    h = (x * (1.0 + scale) + shift).astype(BF16)
    p = jnp.dot(h, w_in_ref[...], preferred_element_type=F32) + b_in_ref[...]
    abuf[CONV_HALO:CONV_HALO + t, :] = p[:, :d] * jax.nn.sigmoid(p[:, d:])
    base = CONV_HALO - (width - 1)
    acc = jnp.zeros((t, d), F32) + b_dw_ref[...]
    for k in range(width):
        acc = acc + w_dw_ref[k:k + 1, :] * abuf[base + k:base + k + t, :]
    abuf[0:CONV_HALO, :] = abuf[t:t + CONV_HALO, :]
    a = _layernorm(acc, cg_ref[...], cb_ref[...])
    a = (a * jax.nn.sigmoid(a)).astype(BF16)
    y = jnp.dot(a, w_out_ref[...], preferred_element_type=F32) + b_out_ref[...]
    o_ref[0] = _layernorm(alpha * x + gate * y, g_ref[...], b_ref[...])


def _conv_mixer(x, mod, layer, w_in, b_in, w_dw, b_dw, cg, cb, w_out, b_out, g, b, alpha):
    bn, s, d = x.shape
    t = min(s, 512)
    width = w_dw.shape[0]
    row = lambda v: v.reshape(1, -1)
    const = lambda shape: pl.BlockSpec(shape, lambda i, j: (0,) * len(shape))
    return pl.pallas_call(
        functools.partial(_conv_kernel, width=width, alpha=alpha),
        grid=(bn, s // t),
        in_specs=[pl.BlockSpec((1, t, d), lambda i, j: (i, j, 0)),
                  pl.BlockSpec((1, 1, 1, 3 * d), lambda i, j: (layer, i, 0, 0)),
                  const((d, 2 * d)), const((1, 2 * d)), const((width, d)), const((1, d)),
                  const((1, d)), const((1, d)), const((d, d)), const((1, d)),
                  const((1, d)), const((1, d))],
        out_specs=pl.BlockSpec((1, t, d), lambda i, j: (i, j, 0)),
        out_shape=jax.ShapeDtypeStruct((bn, s, d), F32),
        scratch_shapes=[pltpu.VMEM((t + CONV_HALO, d), F32)],
        compiler_params=_cparams(("arbitrary", "arbitrary")),
        name="conv_mixer",
    )(x, mod, w_in.astype(BF16), row(b_in), w_dw, row(b_dw), row(cg), row(cb),
      w_out.astype(BF16), row(b_out), row(g), row(b))


def _qkv_kernel(x_ref, mod_ref, w_ref, b_ref, wf_ref, bf_ref, q_ref, k_ref, v_ref, cum_ref, carry,
                *, n_head, q_scale):
    t, d = x_ref.shape[1], x_ref.shape[2]
    hd = d // n_head

    @pl.when(pl.program_id(1) == 0)
    def _():
        carry[...] = jnp.zeros(carry.shape, F32)

    shift, scale, _ = _split_mod(mod_ref[0, 0], d)
    h = (x_ref[0] * (1.0 + scale) + shift).astype(BF16)
    proj = jnp.dot(h, w_ref[...], preferred_element_type=F32) + b_ref[...]
    q = (proj[:, :d] * q_scale).astype(BF16)
    k = proj[:, d:2 * d].astype(BF16)
    v = proj[:, 2 * d:].astype(BF16)
    for i in range(n_head):
        q_ref[0, i] = q[:, i * hd:(i + 1) * hd]
        k_ref[0, i] = k[:, i * hd:(i + 1) * hd]
        v_ref[0, i] = v[:, i * hd:(i + 1) * hd]
    f = jnp.dot(h, wf_ref[...], preferred_element_type=F32) + bf_ref[...]
    log_f = jax.nn.log_sigmoid(f)
    r = lax.broadcasted_iota(jnp.int32, (t, t), 0)
    c = lax.broadcasted_iota(jnp.int32, (t, t), 1)
    tri = jnp.where(c <= r, 1.0, 0.0).astype(BF16)
    hi = log_f.astype(BF16)
    rem = log_f - hi.astype(F32)
    mid = rem.astype(BF16)
    lo = (rem - mid.astype(F32)).astype(BF16)
    cum = (jnp.dot(tri, hi, preferred_element_type=F32)
           + jnp.dot(tri, mid, preferred_element_type=F32)
           + jnp.dot(tri, lo, preferred_element_type=F32)) + carry[0:1, :]
    cum_ref[0] = cum
    carry[0:1, :] = cum[t - 1:t, :]


def _qkv(x, mod, layer, w_in, b_in, n_head):
    bn, s, d = x.shape
    hd = d // n_head
    t = min(s, 512)
    w_qkv = w_in[:, :3 * d].astype(BF16)
    b_qkv = b_in[:3 * d].reshape(1, 3 * d)
    w_f = jnp.pad(w_in[:, 3 * d:], ((0, 0), (0, LANES - n_head))).astype(BF16)
    b_f = jnp.pad(b_in[3 * d:], (0, LANES - n_head)).reshape(1, LANES)
    const = lambda shape: pl.BlockSpec(shape, lambda i, j: (0,) * len(shape))
    head_spec = pl.BlockSpec((1, n_head, t, hd), lambda i, j: (i, 0, j, 0))
    head_shape = jax.ShapeDtypeStruct((bn, n_head, s, hd), BF16)
    return pl.pallas_call(
        functools.partial(_qkv_kernel, n_head=n_head, q_scale=hd ** -0.5),
        grid=(bn, s // t),
        in_specs=[pl.BlockSpec((1, t, d), lambda i, j: (i, j, 0)),
                  pl.BlockSpec((1, 1, 1, 3 * d), lambda i, j: (layer, i, 0, 0)),
                  const((d, 3 * d)), const((1, 3 * d)), const((d, LANES)), const((1, LANES))],
        out_specs=[head_spec, head_spec, head_spec,
                   pl.BlockSpec((1, t, LANES), lambda i, j: (i, j, 0))],
        out_shape=[head_shape, head_shape, head_shape,
                   jax.ShapeDtypeStruct((bn, s, LANES), F32)],
        scratch_shapes=[pltpu.VMEM((SUBLANES, LANES), F32)],
        compiler_params=_cparams(("arbitrary", "arbitrary")),
        name="fox_qkv",
    )(x, mod, w_qkv, b_qkv, w_f, b_f)


def _attn_kernel(q_ref, k_ref, v_ref, cq_ref, ck_ref, o_ref, *, tk):
    tq, hd = q_ref.shape[2], q_ref.shape[3]
    qi = pl.program_id(2)
    q = q_ref[0, 0]
    cq = cq_ref[0, 0]

    def step(kv, carry, masked):
        m, l, acc = carry
        off = pl.multiple_of(kv * tk, tk)
        k = k_ref[0, 0, pl.ds(off, tk), :]
        v = v_ref[0, 0, pl.ds(off, tk), :]
        s = lax.dot_general(q, k, (((1,), (1,)), ((), ())), preferred_element_type=F32)
        s = s + cq - ck_ref[0, 0, :, pl.ds(off, tk)]
        if masked:
            r = lax.broadcasted_iota(jnp.int32, (tq, tk), 0)
            c = lax.broadcasted_iota(jnp.int32, (tq, tk), 1)
            s = jnp.where(c <= r, s, NEG_INF)
        m_new = jnp.maximum(m, jnp.max(s, axis=-1, keepdims=True))
        a = jnp.exp(m - m_new)
        p = jnp.exp(s - m_new)
        l = a * l + jnp.sum(p, axis=-1, keepdims=True)
        acc = a * acc + jnp.dot(p.astype(BF16), v, preferred_element_type=F32)
        return m_new, l, acc

    init = (jnp.full((tq, 1), NEG_INF, F32), jnp.zeros((tq, 1), F32), jnp.zeros((tq, hd), F32))
    carry = lax.fori_loop(0, qi, lambda kv, c: step(kv, c, False), init)
    _, l, acc = step(qi, carry, True)
    o_ref[0, 0] = (acc / l).astype(o_ref.dtype)


def _attention(q, k, v, cq, ck):
    bn, n_head, s, hd = q.shape
    t = min(s, 512)
    return pl.pallas_call(
        functools.partial(_attn_kernel, tk=t),
        grid=(bn, n_head, s // t),
        in_specs=[pl.BlockSpec((1, 1, t, hd), lambda b, h, i: (b, h, i, 0)),
                  pl.BlockSpec((1, 1, s, hd), lambda b, h, i: (b, h, 0, 0)),
                  pl.BlockSpec((1, 1, s, hd), lambda b, h, i: (b, h, 0, 0)),
                  pl.BlockSpec((1, 1, t, 1), lambda b, h, i: (b, h, i, 0)),
                  pl.BlockSpec((1, 1, 1, s), lambda b, h, i: (b, h, 0, 0))],
        out_specs=pl.BlockSpec((1, 1, t, hd), lambda b, h, i: (b, h, i, 0)),
        out_shape=jax.ShapeDtypeStruct((bn, n_head, s, hd), BF16),
        compiler_params=_cparams(("arbitrary", "arbitrary", "arbitrary")),
        name="fox_attention",
    )(q, k, v, cq, ck)


def _attn_out_kernel(o_ref, x_ref, mod_ref, w_ref, b_ref, g_ref, bb_ref, out_ref, *, alpha):
    n_head = o_ref.shape[1]
    d = x_ref.shape[2]
    o = jnp.concatenate([o_ref[0, i] for i in range(n_head)], axis=-1)
    y = jnp.dot(o, w_ref[...], preferred_element_type=F32) + b_ref[...]
    _, _, gate = _split_mod(mod_ref[0, 0], d)
    out_ref[0] = _layernorm(alpha * x_ref[0] + gate * y, g_ref[...], bb_ref[...])


def _attn_out(o, x, mod, layer, w_out, b_out, g, b, alpha):
    bn, s, d = x.shape
    n_head, hd = o.shape[1], o.shape[3]
    t = min(s, 512)
    row = lambda v: v.reshape(1, -1)
    const = lambda shape: pl.BlockSpec(shape, lambda i, j: (0,) * len(shape))
    return pl.pallas_call(
        functools.partial(_attn_out_kernel, alpha=alpha),
        grid=(bn, s // t),
        in_specs=[pl.BlockSpec((1, n_head, t, hd), lambda i, j: (i, 0, j, 0)),
                  pl.BlockSpec((1, t, d), lambda i, j: (i, j, 0)),
                  pl.BlockSpec((1, 1, 1, 3 * d), lambda i, j: (layer, i, 0, 0)),
                  const((d, d)), const((1, d)), const((1, d)), const((1, d))],
        out_specs=pl.BlockSpec((1, t, d), lambda i, j: (i, j, 0)),
        out_shape=jax.ShapeDtypeStruct((bn, s, d), F32),
        compiler_params=_cparams(("arbitrary", "arbitrary")),
        name="fox_out",
    )(o, x, mod, w_out.astype(BF16), row(b_out), row(g), row(b))


def _sort_desc(a):
    a = list(a)
    n = len(a)
    k = 2
    while k <= n:
        j = k // 2
        while j >= 1:
            for i in range(n):
                l = i ^ j
                if l > i:
                    hi, lo = jnp.maximum(a[i], a[l]), jnp.minimum(a[i], a[l])
                    a[i], a[l] = (hi, lo) if (i & k) == 0 else (lo, hi)
            j //= 2
        k *= 2
    return a


def _bitonic_merge_desc(t):
    t = list(t)
    n = len(t)
    j = n // 2
    while j >= 1:
        for i in range(n):
            l = i ^ j
            if l > i:
                t[i], t[l] = jnp.maximum(t[i], t[l]), jnp.minimum(t[i], t[l])
        j //= 2
    return t


def _merge_top(a, b):
    n = len(a)
    t = [a[i] if b[n - 1 - i] is None else jnp.maximum(a[i], b[n - 1 - i]) for i in range(n)]
    return _bitonic_merge_desc(t)


def _top_values(s_ref, tk):
    n_keys = s_ref.shape[0]
    groups = [s_ref[g * SUBLANES:(g + 1) * SUBLANES, :] for g in range(n_keys // SUBLANES)]
    top = _sort_desc(groups[:tk])
    for g0 in range(tk, len(groups), tk):
        top = _merge_top(top, _sort_desc(groups[g0:g0 + tk]))
    shift = SUBLANES // 2
    while shift >= 1:
        top = _merge_top(top, [pltpu.roll(v, shift, 0) for v in top])
        shift //= 2
    return top


def _candidate_counts(v1, v2, tk):
    cand = {}
    for i in range(tk):
        for j in range(tk // (i + 1)):
            cand[i, j] = v1[i] + v2[j]
    half = tk // 2
    top = [cand[0, j] for j in range(tk)]
    lst_a = [cand[1, j] for j in range(half)] + [cand[i, 0] for i in range(tk - 1, half - 1, -1)]
    top = _merge_top(top, _bitonic_merge_desc(lst_a))
    rest = [cand[i, j] for i in range(2, half) for j in range(tk // (i + 1))]
    for g0 in range(0, len(rest), tk):
        grp = rest[g0:g0 + tk]
        if len(grp) == tk:
            grp = _sort_desc(grp)
        else:
            grp = _sort_desc(grp + [jnp.full_like(grp[0], NEG_INF)] * (tk - len(grp)))
        top = _merge_top(top, grp)
    tau = top[tk - 1]
    e1 = [jnp.exp(v1[i] - v1[0]) for i in range(tk)]
    e2 = [jnp.exp(v2[j] - v2[0]) for j in range(tk)]
    cnt, z = [], None
    for i in range(tk):
        c_i, z_i = None, None
        for j in range(tk // (i + 1)):
            sel = cand[i, j] >= tau
            one = jnp.where(sel, 1.0, 0.0)
            w = jnp.where(sel, e2[j], 0.0)
            c_i = one if c_i is None else c_i + one
            z_i = w if z_i is None else z_i + w
        cnt.append(c_i)
        z = e1[i] * z_i if z is None else z + e1[i] * z_i
    return cnt, 1.0 / z


def _route_kernel(x_ref, mod_ref, wq_ref, k1_ref, k2_ref,
                  ht_ref, c_ref, e1_ref, r2_ref, e2_ref,
                  q_scr, s_scr, v_scr, vh_scr, cs_scr, *, n_head, tk):
    t, d = x_ref.shape
    n_keys = k1_ref.shape[0]
    shift, scale, _ = _split_mod(mod_ref[0, 0], d)
    h = x_ref[...] * (1.0 + scale) + shift
    ht = h.T.astype(BF16)
    ht_ref[...] = ht
    q_scr[...] = jnp.dot(wq_ref[...], ht, preferred_element_type=F32).astype(BF16)

    def scores_and_top(hh, carry):
        for half, k_ref in enumerate((k1_ref, k2_ref)):
            off = pl.multiple_of((2 * hh + half) * n_keys, n_keys)
            s_scr[hh, half] = jnp.dot(k_ref[...], q_scr[pl.ds(off, n_keys), :],
                                      preferred_element_type=F32)
            top = _top_values(s_scr.at[hh, half], tk)
            for i in range(tk):
                v_scr[hh, half, i] = top[i]
                vh_scr[half, i, pl.ds(hh, 1), :] = top[i][0:1, :]
        return carry

    lax.fori_loop(0, n_head, scores_and_top, 0)

    v1 = [vh_scr[0, i] for i in range(tk)]
    v2 = [vh_scr[1, i] for i in range(tk)]
    cnt, inv_z = _candidate_counts(v1, v2, tk)
    cs_scr[0] = cnt[0]
    for i in range(tk):
        nxt = cnt[i + 1] if i + 1 < tk else jnp.zeros_like(cnt[i])
        cs_scr[1 + i] = nxt - cnt[i]
    cs_scr[tk + 1] = inv_z

    def dense(hh, carry):
        def row(idx):
            return jnp.broadcast_to(cs_scr[idx, pl.ds(hh, 1), :], (SUBLANES, t))
        cnt0 = row(0)
        dlt = [row(1 + i) for i in range(tk)]
        inv = row(tk + 1)
        v1h = [v_scr[hh, 0, i] for i in range(tk)]
        v2h = [v_scr[hh, 1, i] for i in range(tk)]
        for g in range(n_keys // SUBLANES):
            rows = slice(g * SUBLANES, (g + 1) * SUBLANES)
            s1 = s_scr[hh, 0, rows, :]
            s2 = s_scr[hh, 1, rows, :]
            c = cnt0
            r2 = jnp.zeros((SUBLANES, t), F32)
            for i in range(tk):
                c = c + jnp.where(v1h[i] > s1, dlt[i], 0.0)
                r2 = r2 + jnp.where(v2h[i] > s2, 1.0, 0.0)
            c_ref[hh, rows, :] = c
            e1_ref[hh, rows, :] = jnp.exp(s1 - v1h[0])
            r2_ref[hh, rows, :] = r2.astype(BF16)
            e2_ref[hh, rows, :] = (jnp.exp(s2 - v2h[0]) * inv).astype(BF16)
        return carry

    lax.fori_loop(0, n_head, dense, 0)


def _peer_route(x2, mod, layer, seq, wq_t, k1, k2, n_head):
    n_tok, d = x2.shape
    n_keys = k1.shape[0]
    t = min(n_tok, 256)
    tk = PEER_TOPK
    const = lambda shape: pl.BlockSpec(shape, lambda i: (0,) * len(shape))
    tok3 = pl.BlockSpec((n_head, n_keys, t), lambda i: (0, 0, i))
    return pl.pallas_call(
        functools.partial(_route_kernel, n_head=n_head, tk=tk),
        grid=(n_tok // t,),
        in_specs=[pl.BlockSpec((t, d), lambda i: (i, 0)),
                  pl.BlockSpec((1, 1, 1, 3 * d), lambda i: (layer, (i * t) // seq, 0, 0)),
                  const(wq_t.shape), const(k1.shape), const(k2.shape)],
        out_specs=[pl.BlockSpec((d, t), lambda i: (0, i)), tok3, tok3, tok3, tok3],
        out_shape=[jax.ShapeDtypeStruct((d, n_tok), BF16),
                   jax.ShapeDtypeStruct((n_head, n_keys, n_tok), F32),
                   jax.ShapeDtypeStruct((n_head, n_keys, n_tok), F32),
                   jax.ShapeDtypeStruct((n_head, n_keys, n_tok), BF16),
                   jax.ShapeDtypeStruct((n_head, n_keys, n_tok), BF16)],
        scratch_shapes=[pltpu.VMEM((2 * n_head * n_keys, t), BF16),
                        pltpu.VMEM((n_head, 2, n_keys, t), F32),
                        pltpu.VMEM((n_head, 2, tk, SUBLANES, t), F32),
                        pltpu.VMEM((2, tk, SUBLANES, t), F32),
                        pltpu.VMEM((tk + 2, SUBLANES, t), F32)],
        compiler_params=_cparams(("arbitrary",)),
        name="peer_route",
    )(x2, mod, wq_t, k1, k2)


def _peer_dense_kernel(ht_ref, u_ref, vt_ref, c_ref, e1_ref, r2_ref, e2_ref, x_ref, mod_ref,
                       g_ref, b_ref, o_ref, a_scr, w_scr, acc_scr, *, n_head, n_keys, alpha):
    t, d = x_ref.shape
    e_blk = u_ref.shape[0]
    n_a = e_blk // n_keys
    pack = 2 * SUBLANES
    j = pl.program_id(1)

    @pl.when(j == 0)
    def _():
        acc_scr[...] = jnp.zeros(acc_scr.shape, F32)

    a_scr[...] = jnp.dot(u_ref[...], ht_ref[...], preferred_element_type=F32)

    def per_key(a, carry):
        ag = j * n_a + a
        off = pl.multiple_of(a * n_keys, n_keys)
        pre = a_scr[pl.ds(off, n_keys), :]
        act = (0.5 * pre * (1.0 + lax.erf(pre * INV_SQRT2))).astype(BF16)
        gate = [jnp.zeros((pack, t), BF16) for _ in range(n_keys // pack)]
        for hh in range(n_head):
            cnt = jnp.broadcast_to(c_ref[hh, pl.ds(ag, 1), :], (pack, t)).astype(BF16)
            e1 = jnp.broadcast_to(e1_ref[hh, pl.ds(ag, 1), :], (pack, t)).astype(BF16)
            for bc in range(n_keys // pack):
                rows = slice(bc * pack, (bc + 1) * pack)
                sel = jnp.where(r2_ref[hh, rows, :] < cnt, e2_ref[hh, rows, :], jnp.zeros((), BF16))
                gate[bc] = gate[bc] + sel * e1
        for bc in range(n_keys // pack):
            w_scr[pl.ds(off + bc * pack, pack), :] = act[bc * pack:(bc + 1) * pack, :] * gate[bc]
        return carry

    lax.fori_loop(0, n_a, per_key, 0)
    acc_scr[...] += jnp.dot(vt_ref[...], w_scr[...], preferred_element_type=F32)

    @pl.when(j == pl.num_programs(1) - 1)
    def _():
        _, _, gate_mod = _split_mod(mod_ref[0, 0], d)
        y = acc_scr[...].T
        o_ref[...] = _layernorm(alpha * x_ref[...] + gate_mod * y, g_ref[...], b_ref[...])


def _peer_dense(x2, mod, layer, seq, ht, c, e1, r2, e2, u_bf, vt_bf, g, b, alpha):
    n_tok, d = x2.shape
    n_head, n_keys, _ = c.shape
    n_exp = u_bf.shape[0]
    t = min(n_tok, 512)
    e_blk = min(n_exp, 1024)
    tok3 = pl.BlockSpec((n_head, n_keys, t), lambda i, j: (0, 0, i))
    row = lambda v: v.reshape(1, -1)
    return pl.pallas_call(
        functools.partial(_peer_dense_kernel, n_head=n_head, n_keys=n_keys, alpha=alpha),
        grid=(n_tok // t, n_exp // e_blk),
        in_specs=[pl.BlockSpec((d, t), lambda i, j: (0, i)),
                  pl.BlockSpec((e_blk, d), lambda i, j: (j, 0)),
                  pl.BlockSpec((d, e_blk), lambda i, j: (0, j)),
                  tok3, tok3, tok3, tok3,
                  pl.BlockSpec((t, d), lambda i, j: (i, 0)),
                  pl.BlockSpec((1, 1, 1, 3 * d), lambda i, j: (layer, (i * t) // seq, 0, 0)),
                  pl.BlockSpec((1, d), lambda i, j: (0, 0)),
                  pl.BlockSpec((1, d), lambda i, j: (0, 0))],
        out_specs=pl.BlockSpec((t, d), lambda i, j: (i, 0)),
        out_shape=jax.ShapeDtypeStruct((n_tok, d), F32),
        scratch_shapes=[pltpu.VMEM((e_blk, t), F32),
                        pltpu.VMEM((e_blk, t), BF16),
                        pltpu.VMEM((d, t), F32)],
        compiler_params=_cparams(("arbitrary", "arbitrary")),
        name="peer_dense",
    )(ht, u_bf, vt_bf, c, e1, r2, e2, x2, mod, row(g), row(b))


def _peer_layer(x, mod, layer, wq, k1, k2, u, v, g, b, alpha):
    bn, s, d = x.shape
    n_keys, half_key = k1.shape
    n_head = wq.shape[1] // (2 * half_key)
    x2 = x.reshape(bn * s, d)
    ht, c, e1, r2, e2 = _peer_route(x2, mod, layer, s, wq.T.astype(BF16), k1.astype(BF16),
                                    k2.astype(BF16), n_head)
    out = _peer_dense(x2, mod, layer, s, ht, c, e1, r2, e2, u.astype(BF16), v.T.astype(BF16),
                      g, b, alpha)
    return out.reshape(bn, s, d)


def kernel(x, c, ada_mix_w, ada_mix_b, ln_mix_g, ln_mix_b, conv_in_w, conv_in_b, conv_dw_w, conv_dw_b, conv_ln_g, conv_ln_b, conv_out_w, conv_out_b, attn_in_w, attn_in_b, attn_out_w, attn_out_b, ada_ffn_w, ada_ffn_b, ln_ffn_g, ln_ffn_b, peer_query_w, peer_sub_keys_1, peer_sub_keys_2, peer_expert_u, peer_expert_v):
    depth = ada_mix_w.shape[0]
    d = x.shape[-1]
    n_attn_head = attn_in_w.shape[-1] - 3 * d
    alpha = (2 * depth) ** 0.25
    mod_mix = _adaln(c, ada_mix_w, ada_mix_b)
    mod_ffn = _adaln(c, ada_ffn_w, ada_ffn_b)
    for i in range(depth):
        j = i // 2
        if i % 2 == 0:
            x = _conv_mixer(x, mod_mix, i, conv_in_w[j], conv_in_b[j], conv_dw_w[j], conv_dw_b[j],
                            conv_ln_g[j], conv_ln_b[j], conv_out_w[j], conv_out_b[j],
                            ln_mix_g[i], ln_mix_b[i], alpha)
        else:
            q, k, v, cum = _qkv(x, mod_mix, i, attn_in_w[j], attn_in_b[j], n_attn_head)
            cum_t = jnp.swapaxes(cum[:, :, :n_attn_head], 1, 2)
            o = _attention(q, k, v, cum_t[..., None], cum_t[:, :, None, :])
            x = _attn_out(o, x, mod_mix, i, attn_out_w[j], attn_out_b[j],
                          ln_mix_g[i], ln_mix_b[i], alpha)
        x = _peer_layer(x, mod_ffn, i, peer_query_w[i], peer_sub_keys_1[i], peer_sub_keys_2[i],
                        peer_expert_u[i], peer_expert_v[i], ln_ffn_g[i], ln_ffn_b[i], alpha)
    return x
```

```python
import functools
import math

import jax
import jax.numpy as jnp
from jax import lax
from jax.experimental import pallas as pl
from jax.experimental.pallas import tpu as pltpu

F32 = jnp.float32
BF16 = jnp.bfloat16

LN_EPS = 1e-5
CONV_HALO = 32
PEER_TOPK = 16
V7X_VMEM_LIMIT_BYTES = 56 * 1024 * 1024
SUBLANES = 8
LANES = 128
INV_SQRT2 = 1.0 / math.sqrt(2.0)
NEG_INF = float("-inf")


def _cparams(sem):
    return pltpu.CompilerParams(dimension_semantics=sem, vmem_limit_bytes=V7X_VMEM_LIMIT_BYTES)


def _layernorm(z, g, b):
    mu = jnp.mean(z, axis=-1, keepdims=True)
    zc = z - mu
    var = jnp.mean(zc * zc, axis=-1, keepdims=True)
    return zc * lax.rsqrt(var + LN_EPS) * g + b


def _split_mod(mod, d):
    return mod[:, :d], mod[:, d:2 * d], mod[:, 2 * d:]


def _ada_kernel(c_ref, w_ref, b_ref, o_ref):
    c = c_ref[...]
    sc = (c * jax.nn.sigmoid(c)).astype(BF16)
    o_ref[0] = jnp.dot(sc, w_ref[0].astype(BF16), preferred_element_type=F32) + b_ref[0]


def _adaln(c, w, b):
    n_layer, d, d3 = w.shape
    bn = c.shape[0]
    tn = min(d3, 1024)
    out = pl.pallas_call(
        _ada_kernel,
        grid=(n_layer, d3 // tn),
        in_specs=[pl.BlockSpec((bn, d), lambda l, j: (0, 0)),
                  pl.BlockSpec((1, d, tn), lambda l, j: (l, 0, j)),
                  pl.BlockSpec((1, 1, tn), lambda l, j: (l, 0, j))],
        out_specs=pl.BlockSpec((1, bn, tn), lambda l, j: (l, 0, j)),
        out_shape=jax.ShapeDtypeStruct((n_layer, bn, d3), F32),
        compiler_params=_cparams(("arbitrary", "arbitrary")),
        name="adaln",
    )(c, w, b.reshape(n_layer, 1, d3))
    return out.reshape(n_layer, bn, 1, d3)


def _conv_kernel(x_ref, mod_ref, w_in_ref, b_in_ref, w_dw_ref, b_dw_ref, cg_ref, cb_ref,
                 w_out_ref, b_out_ref, g_ref, b_ref, o_ref, abuf, shbuf, *, width, alpha):
    t, d = x_ref.shape[1], x_ref.shape[2]

    @pl.when(pl.program_id(1) == 0)
    def _():
        abuf[0:CONV_HALO, :] = jnp.zeros((CONV_HALO, d), F32)

    x = x_ref[0]
    shift, scale, gate = _split_mod(mod_ref[0, 0], d)
    h = (x * (1.0 + scale) + shift).astype(BF16)
    p = jnp.dot(h, w_in_ref[...], preferred_element_type=F32) + b_in_ref[...]
    abuf[CONV_HALO:CONV_HALO + t, :] = p[:, :d] * jax.nn.sigmoid(p[:, d:])
    base = CONV_HALO - (width - 1)
    n_sh = shbuf.shape[1]
    for s in range(1, SUBLANES):
        shbuf[s - 1] = abuf[s:s + n_sh, :]
    acc = jnp.zeros((t, d), F32) + b_dw_ref[...]
    for k in range(width):
        row0, s = (base + k) // SUBLANES * SUBLANES, (base + k) % SUBLANES
        tap = abuf[row0:row0 + t, :] if s == 0 else shbuf[s - 1, row0:row0 + t, :]
        acc = acc + w_dw_ref[k:k + 1, :] * tap
    abuf[0:CONV_HALO, :] = abuf[t:t + CONV_HALO, :]
    a = _layernorm(acc, cg_ref[...], cb_ref[...])
    a = (a * jax.nn.sigmoid(a)).astype(BF16)
    y = jnp.dot(a, w_out_ref[...], preferred_element_type=F32) + b_out_ref[...]
    o_ref[0] = _layernorm(alpha * x + gate * y, g_ref[...], b_ref[...])


def _conv_mixer(x, mod, layer, w_in, b_in, w_dw, b_dw, cg, cb, w_out, b_out, g, b, alpha):
    bn, s, d = x.shape
    t = min(s, 512)
    width = w_dw.shape[0]
    row = lambda v: v.reshape(1, -1)
    const = lambda shape: pl.BlockSpec(shape, lambda i, j: (0,) * len(shape))
    return pl.pallas_call(
        functools.partial(_conv_kernel, width=width, alpha=alpha),
        grid=(bn, s // t),
        in_specs=[pl.BlockSpec((1, t, d), lambda i, j: (i, j, 0)),
                  pl.BlockSpec((1, 1, 1, 3 * d), lambda i, j: (layer, i, 0, 0)),
                  const((d, 2 * d)), const((1, 2 * d)), const((width, d)), const((1, d)),
                  const((1, d)), const((1, d)), const((d, d)), const((1, d)),
                  const((1, d)), const((1, d))],
        out_specs=pl.BlockSpec((1, t, d), lambda i, j: (i, j, 0)),
        out_shape=jax.ShapeDtypeStruct((bn, s, d), F32),
        scratch_shapes=[pltpu.VMEM((t + CONV_HALO, d), F32),
                        pltpu.VMEM((SUBLANES - 1, t + CONV_HALO - SUBLANES, d), F32)],
        compiler_params=_cparams(("arbitrary", "arbitrary")),
        name="conv_mixer",
    )(x, mod, w_in.astype(BF16), row(b_in), w_dw, row(b_dw), row(cg), row(cb),
      w_out.astype(BF16), row(b_out), row(g), row(b))


def _qkv_kernel(x_ref, mod_ref, w_ref, b_ref, wf_ref, bf_ref, q_ref, k_ref, v_ref, cum_ref, carry,
                *, q_scale):
    t, d = x_ref.shape[1], x_ref.shape[2]

    @pl.when(pl.program_id(1) == 0)
    def _():
        carry[...] = jnp.zeros(carry.shape, F32)

    shift, scale, _ = _split_mod(mod_ref[0, 0], d)
    h = (x_ref[0] * (1.0 + scale) + shift).astype(BF16)
    proj = jnp.dot(h, w_ref[...], preferred_element_type=F32) + b_ref[...]
    q_ref[0] = (proj[:, :d] * q_scale).astype(BF16)
    k_ref[0] = proj[:, d:2 * d].astype(BF16)
    v_ref[0] = proj[:, 2 * d:].astype(BF16)
    f = jnp.dot(h, wf_ref[...], preferred_element_type=F32) + bf_ref[...]
    log_f = jax.nn.log_sigmoid(f)
    r = lax.broadcasted_iota(jnp.int32, (t, t), 0)
    c = lax.broadcasted_iota(jnp.int32, (t, t), 1)
    tri = jnp.where(c <= r, 1.0, 0.0).astype(BF16)
    hi = log_f.astype(BF16)
    rem = log_f - hi.astype(F32)
    mid = rem.astype(BF16)
    lo = (rem - mid.astype(F32)).astype(BF16)
    cum = (jnp.dot(tri, hi, preferred_element_type=F32)
           + jnp.dot(tri, mid, preferred_element_type=F32)
           + jnp.dot(tri, lo, preferred_element_type=F32)) + carry[0:1, :]
    cum_ref[0] = cum
    carry[0:1, :] = cum[t - 1:t, :]


def _qkv(x, mod, layer, w_in, b_in, n_head):
    bn, s, d = x.shape
    hd = d // n_head
    t = min(s, 512)
    w_qkv = w_in[:, :3 * d].astype(BF16)
    b_qkv = b_in[:3 * d].reshape(1, 3 * d)
    w_f = jnp.pad(w_in[:, 3 * d:], ((0, 0), (0, LANES - n_head))).astype(BF16)
    b_f = jnp.pad(b_in[3 * d:], (0, LANES - n_head)).reshape(1, LANES)
    const = lambda shape: pl.BlockSpec(shape, lambda i, j: (0,) * len(shape))
    head_spec = pl.BlockSpec((1, t, d), lambda i, j: (i, j, 0))
    head_shape = jax.ShapeDtypeStruct((bn, s, d), BF16)
    return pl.pallas_call(
        functools.partial(_qkv_kernel, q_scale=hd ** -0.5),
        grid=(bn, s // t),
        in_specs=[pl.BlockSpec((1, t, d), lambda i, j: (i, j, 0)),
                  pl.BlockSpec((1, 1, 1, 3 * d), lambda i, j: (layer, i, 0, 0)),
                  const((d, 3 * d)), const((1, 3 * d)), const((d, LANES)), const((1, LANES))],
        out_specs=[head_spec, head_spec, head_spec,
                   pl.BlockSpec((1, t, LANES), lambda i, j: (i, j, 0))],
        out_shape=[head_shape, head_shape, head_shape,
                   jax.ShapeDtypeStruct((bn, s, LANES), F32)],
        scratch_shapes=[pltpu.VMEM((SUBLANES, LANES), F32)],
        compiler_params=_cparams(("arbitrary", "arbitrary")),
        name="fox_qkv",
    )(x, mod, w_qkv, b_qkv, w_f, b_f)


def _attn_kernel(q_ref, k_ref, v_ref, cq_ref, ck_ref, o_ref, vaug, *, tk, hd):
    tq = q_ref.shape[1]
    n_sub = LANES // hd
    qi = pl.program_id(2)

    @pl.when(qi == 0)
    def _():
        vaug[:, 0:LANES] = v_ref[0]
        lane = lax.broadcasted_iota(jnp.int32, (vaug.shape[0], LANES), 1)
        vaug[:, LANES:2 * LANES] = jnp.where(lane == 0, 1.0, 0.0).astype(vaug.dtype)

    q2 = q_ref[0]
    lane = lax.broadcasted_iota(jnp.int32, (tq, LANES), 1)
    qh = [jnp.where((lane >= j * hd) & (lane < (j + 1) * hd), q2, jnp.zeros((), q2.dtype))
          for j in range(n_sub)]
    cq = [cq_ref[0, j] for j in range(n_sub)]

    def step(kv, carry, masked):
        off = pl.multiple_of(kv * tk, tk)
        k = k_ref[0, pl.ds(off, tk), :]
        va = vaug[pl.ds(off, tk), :]
        out = []
        for j in range(n_sub):
            m, acc = carry[j]
            z = lax.dot_general(qh[j], k, (((1,), (1,)), ((), ())), preferred_element_type=F32)
            z = z - ck_ref[0, j, :, pl.ds(off, tk)]
            if masked:
                r = lax.broadcasted_iota(jnp.int32, (tq, tk), 0)
                c = lax.broadcasted_iota(jnp.int32, (tq, tk), 1)
                z = jnp.where(c <= r, z, NEG_INF)
            m_new = jnp.maximum(m, jnp.max(z, axis=-1, keepdims=True) + cq[j])
            a = jnp.exp(m - m_new)
            p = jnp.exp((z - (m_new - cq[j])).astype(BF16))
            acc = a * acc + jnp.dot(p, va, preferred_element_type=F32)
            out.append((m_new, acc))
        return tuple(out)

    init = tuple((jnp.full((tq, 1), NEG_INF, F32), jnp.zeros((tq, 2 * LANES), F32))
                 for _ in range(n_sub))
    carry = lax.fori_loop(0, qi, lambda kv, c: step(kv, c, False), init)
    res = step(qi, carry, True)
    o = None
    for j in range(n_sub):
        acc = res[j][1]
        o_j = acc[:, :LANES] / acc[:, LANES:LANES + 1]
        o = o_j if o is None else jnp.where(lane >= j * hd, o_j, o)
    o_ref[0] = o.astype(o_ref.dtype)


def _attention(q, k, v, cq, ck, hd):
    bn, s, d = q.shape
    n_sub = LANES // hd
    t = min(s, 512)
    return pl.pallas_call(
        functools.partial(_attn_kernel, tk=t, hd=hd),
        grid=(bn, d // LANES, s // t),
        in_specs=[pl.BlockSpec((1, t, LANES), lambda b, h, i: (b, i, h)),
                  pl.BlockSpec((1, s, LANES), lambda b, h, i: (b, 0, h)),
                  pl.BlockSpec((1, s, LANES), lambda b, h, i: (b, 0, h)),
                  pl.BlockSpec((1, n_sub, t, 1), lambda b, h, i: (b, h, i, 0)),
                  pl.BlockSpec((1, n_sub, 1, s), lambda b, h, i: (b, h, 0, 0))],
        out_specs=pl.BlockSpec((1, t, LANES), lambda b, h, i: (b, i, h)),
        out_shape=jax.ShapeDtypeStruct((bn, s, d), BF16),
        scratch_shapes=[pltpu.VMEM((s, 2 * LANES), BF16)],
        compiler_params=_cparams(("arbitrary", "arbitrary", "arbitrary")),
        name="fox_attention",
    )(q, k, v, cq, ck)


def _attn_out_kernel(o_ref, x_ref, mod_ref, w_ref, b_ref, g_ref, bb_ref, out_ref, *, alpha):
    d = x_ref.shape[2]
    y = jnp.dot(o_ref[0], w_ref[...], preferred_element_type=F32) + b_ref[...]
    _, _, gate = _split_mod(mod_ref[0, 0], d)
    out_ref[0] = _layernorm(alpha * x_ref[0] + gate * y, g_ref[...], bb_ref[...])


def _attn_out(o, x, mod, layer, w_out, b_out, g, b, alpha):
    bn, s, d = x.shape
    t = min(s, 512)
    row = lambda v: v.reshape(1, -1)
    const = lambda shape: pl.BlockSpec(shape, lambda i, j: (0,) * len(shape))
    return pl.pallas_call(
        functools.partial(_attn_out_kernel, alpha=alpha),
        grid=(bn, s // t),
        in_specs=[pl.BlockSpec((1, t, d), lambda i, j: (i, j, 0)),
                  pl.BlockSpec((1, t, d), lambda i, j: (i, j, 0)),
                  pl.BlockSpec((1, 1, 1, 3 * d), lambda i, j: (layer, i, 0, 0)),
                  const((d, d)), const((1, d)), const((1, d)), const((1, d))],
        out_specs=pl.BlockSpec((1, t, d), lambda i, j: (i, j, 0)),
        out_shape=jax.ShapeDtypeStruct((bn, s, d), F32),
        compiler_params=_cparams(("arbitrary", "arbitrary")),
        name="fox_out",
    )(o, x, mod, w_out.astype(BF16), row(b_out), row(g), row(b))


def _sort_desc(a):
    a = list(a)
    n = len(a)
    k = 2
    while k <= n:
        j = k // 2
        while j >= 1:
            for i in range(n):
                l = i ^ j
                if l > i:
                    hi, lo = jnp.maximum(a[i], a[l]), jnp.minimum(a[i], a[l])
                    a[i], a[l] = (hi, lo) if (i & k) == 0 else (lo, hi)
            j //= 2
        k *= 2
    return a


def _bitonic_merge_desc(t):
    t = list(t)
    n = len(t)
    j = n // 2
    while j >= 1:
        for i in range(n):
            l = i ^ j
            if l > i:
                t[i], t[l] = jnp.maximum(t[i], t[l]), jnp.minimum(t[i], t[l])
        j //= 2
    return t


def _merge_top(a, b):
    n = len(a)
    t = [a[i] if b[n - 1 - i] is None else jnp.maximum(a[i], b[n - 1 - i]) for i in range(n)]
    return _bitonic_merge_desc(t)


def _top_values(s_ref, tk):
    n_keys = s_ref.shape[0]
    groups = [s_ref[g * SUBLANES:(g + 1) * SUBLANES, :] for g in range(n_keys // SUBLANES)]
    top = _sort_desc(groups[:tk])
    for g0 in range(tk, len(groups), tk):
        top = _merge_top(top, _sort_desc(groups[g0:g0 + tk]))
    shift = SUBLANES // 2
    while shift >= 1:
        top = _merge_top(top, [pltpu.roll(v, shift, 0) for v in top])
        shift //= 2
    return top


def _candidate_counts(v1, v2, tk):
    cand = {}
    for i in range(tk):
        for j in range(tk // (i + 1)):
            cand[i, j] = v1[i] + v2[j]
    half = tk // 2
    top = [cand[0, j] for j in range(tk)]
    lst_a = [cand[1, j] for j in range(half)] + [cand[i, 0] for i in range(tk - 1, half - 1, -1)]
    top = _merge_top(top, _bitonic_merge_desc(lst_a))
    rest = [cand[i, j] for i in range(2, half) for j in range(tk // (i + 1))]
    for g0 in range(0, len(rest), tk):
        grp = rest[g0:g0 + tk]
        if len(grp) == tk:
            grp = _sort_desc(grp)
        else:
            grp = _sort_desc(grp + [jnp.full_like(grp[0], NEG_INF)] * (tk - len(grp)))
        top = _merge_top(top, grp)
    tau = top[tk - 1]
    e1 = [jnp.exp(v1[i] - v1[0]) for i in range(tk)]
    e2 = [jnp.exp(v2[j] - v2[0]) for j in range(tk)]
    cnt, z = [], None
    for i in range(tk):
        c_i, z_i = None, None
        for j in range(tk // (i + 1)):
            sel = cand[i, j] >= tau
            one = jnp.where(sel, 1.0, 0.0)
            w = jnp.where(sel, e2[j], 0.0)
            c_i = one if c_i is None else c_i + one
            z_i = w if z_i is None else z_i + w
        cnt.append(c_i)
        z = e1[i] * z_i if z is None else z + e1[i] * z_i
    return cnt, 1.0 / z


def _route_kernel(x_ref, mod_ref, wq_ref, k1_ref, k2_ref,
                  ht_ref, c_ref, e1_ref, r2_ref, e2_ref,
                  q_scr, s_scr, v_scr, vh_scr, cs_scr, *, n_head, tk):
    t, d = x_ref.shape
    n_keys = k1_ref.shape[0]
    shift, scale, _ = _split_mod(mod_ref[0, 0], d)
    h = x_ref[...] * (1.0 + scale) + shift
    ht = h.T.astype(BF16)
    ht_ref[...] = ht
    q_scr[...] = jnp.dot(wq_ref[...], ht, preferred_element_type=F32).astype(BF16)

    def scores_and_top(hh, carry):
        for half, k_ref in enumerate((k1_ref, k2_ref)):
            off = pl.multiple_of((2 * hh + half) * n_keys, n_keys)
            s_scr[hh, half] = jnp.dot(k_ref[...], q_scr[pl.ds(off, n_keys), :],
                                      preferred_element_type=F32)
            top = _top_values(s_scr.at[hh, half], tk)
            for i in range(tk):
                v_scr[hh, half, i] = top[i]
                vh_scr[half, i, pl.ds(hh, 1), :] = top[i][0:1, :]
        return carry

    lax.fori_loop(0, n_head, scores_and_top, 0)

    v1 = [vh_scr[0, i] for i in range(tk)]
    v2 = [vh_scr[1, i] for i in range(tk)]
    cnt, inv_z = _candidate_counts(v1, v2, tk)
    cs_scr[0] = cnt[0]
    for i in range(tk):
        nxt = cnt[i + 1] if i + 1 < tk else jnp.zeros_like(cnt[i])
        cs_scr[1 + i] = nxt - cnt[i]
    cs_scr[tk + 1] = inv_z

    def dense(hh, carry):
        def row(idx):
            return jnp.broadcast_to(cs_scr[idx, pl.ds(hh, 1), :], (SUBLANES, t))
        cnt0 = row(0)
        dlt = [row(1 + i) for i in range(tk)]
        inv = row(tk + 1)
        v1h = [v_scr[hh, 0, i] for i in range(tk)]
        v2h = [v_scr[hh, 1, i] for i in range(tk)]
        for g in range(n_keys // SUBLANES):
            rows = slice(g * SUBLANES, (g + 1) * SUBLANES)
            s1 = s_scr[hh, 0, rows, :]
            s2 = s_scr[hh, 1, rows, :]
            c = cnt0
            r2 = jnp.zeros((SUBLANES, t), F32)
            for i in range(tk):
                c = c + jnp.where(v1h[i] > s1, dlt[i], 0.0)
                r2 = r2 + jnp.where(v2h[i] > s2, 1.0, 0.0)
            c_ref[hh, rows, :] = c
            e1_ref[hh, rows, :] = jnp.exp(s1 - v1h[0])
            r2_ref[hh, rows, :] = r2.astype(BF16)
            e2_ref[hh, rows, :] = (jnp.exp(s2 - v2h[0]) * inv).astype(BF16)
        return carry

    lax.fori_loop(0, n_head, dense, 0)


def _peer_route(x2, mod, layer, seq, wq_t, k1, k2, n_head):
    n_tok, d = x2.shape
    n_keys = k1.shape[0]
    t = min(n_tok, 256)
    tk = PEER_TOPK
    const = lambda shape: pl.BlockSpec(shape, lambda i: (0,) * len(shape))
    tok3 = pl.BlockSpec((n_head, n_keys, t), lambda i: (0, 0, i))
    return pl.pallas_call(
        functools.partial(_route_kernel, n_head=n_head, tk=tk),
        grid=(n_tok // t,),
        in_specs=[pl.BlockSpec((t, d), lambda i: (i, 0)),
                  pl.BlockSpec((1, 1, 1, 3 * d), lambda i: (layer, (i * t) // seq, 0, 0)),
                  const(wq_t.shape), const(k1.shape), const(k2.shape)],
        out_specs=[pl.BlockSpec((d, t), lambda i: (0, i)), tok3, tok3, tok3, tok3],
        out_shape=[jax.ShapeDtypeStruct((d, n_tok), BF16),
                   jax.ShapeDtypeStruct((n_head, n_keys, n_tok), F32),
                   jax.ShapeDtypeStruct((n_head, n_keys, n_tok), F32),
                   jax.ShapeDtypeStruct((n_head, n_keys, n_tok), BF16),
                   jax.ShapeDtypeStruct((n_head, n_keys, n_tok), BF16)],
        scratch_shapes=[pltpu.VMEM((2 * n_head * n_keys, t), BF16),
                        pltpu.VMEM((n_head, 2, n_keys, t), F32),
                        pltpu.VMEM((n_head, 2, tk, SUBLANES, t), F32),
                        pltpu.VMEM((2, tk, SUBLANES, t), F32),
                        pltpu.VMEM((tk + 2, SUBLANES, t), F32)],
        compiler_params=_cparams(("arbitrary",)),
        name="peer_route",
    )(x2, mod, wq_t, k1, k2)


def _peer_dense_kernel(ht_ref, u_ref, un_ref, vt_ref, vp_ref, c_ref, e1_ref, r2_ref, e2_ref,
                       x_ref, mod_ref, g_ref, b_ref, o_ref, a0, a1, w0, w1, acc_scr,
                       *, n_head, n_keys, alpha):
    t, d = x_ref.shape
    chunk = a0.shape[0]
    n_c = u_ref.shape[0] // chunk
    n_a = chunk // n_keys
    pack = 2 * SUBLANES
    k = pl.program_id(1)

    def u_matmul(u_rows, dst):
        dst[...] = jnp.dot(u_rows, ht_ref[...], preferred_element_type=F32)

    def v_matmul(vt_chunk, w_scr):
        acc_scr[...] += jnp.dot(vt_chunk, w_scr[...], preferred_element_type=F32)

    @pl.when(k == 0)
    def _():
        w1[...] = jnp.zeros(w1.shape, w1.dtype)
        acc_scr[...] = jnp.zeros(acc_scr.shape, F32)
        u_matmul(u_ref[0:chunk, :], a0)

    def gate_pass(a_scr, w_scr, c):
        for a in range(n_a):
            ag = (k * n_c + c) * n_a + a
            pre = a_scr[a * n_keys:(a + 1) * n_keys, :]
            act = (0.5 * pre * (1.0 + lax.erf(pre * INV_SQRT2))).astype(BF16)
            gate = [jnp.zeros((pack, t), BF16) for _ in range(n_keys // pack)]
            for hh in range(n_head):
                cnt = jnp.broadcast_to(c_ref[hh, pl.ds(ag, 1), :], (pack, t)).astype(BF16)
                e1 = jnp.broadcast_to(e1_ref[hh, pl.ds(ag, 1), :], (pack, t)).astype(BF16)
                for bc in range(n_keys // pack):
                    rows = slice(bc * pack, (bc + 1) * pack)
                    sel = jnp.where(r2_ref[hh, rows, :] < cnt, e2_ref[hh, rows, :],
                                    jnp.zeros((), BF16))
                    gate[bc] = gate[bc] + sel * e1
            for bc in range(n_keys // pack):
                r0 = a * n_keys + bc * pack
                w_scr[r0:r0 + pack, :] = act[bc * pack:(bc + 1) * pack, :] * gate[bc]

    def u_rows(c):
        return u_ref[pl.ds(pl.multiple_of(c * chunk, chunk), chunk), :]

    def stage_pair(c, first, last):
        u_matmul(u_rows(c + 1), a1)
        gate_pass(a0, w0, c)
        v_matmul(vp_ref[0] if first else vt_ref[c - 1], w1)
        u_matmul(un_ref[...] if last else u_rows(c + 2), a0)
        gate_pass(a1, w1, c + 1)
        v_matmul(vt_ref[c], w0)

    stage_pair(0, True, n_c == 2)
    if n_c > 4:
        def body(p, carry):
            stage_pair(2 * p, False, False)
            return carry
        lax.fori_loop(1, n_c // 2 - 1, body, 0)
    if n_c > 2:
        stage_pair(n_c - 2, False, True)

    @pl.when(k == pl.num_programs(1) - 1)
    def _():
        v_matmul(vt_ref[n_c - 1], w1)
        _, _, gate_mod = _split_mod(mod_ref[0, 0], d)
        y = acc_scr[...].T
        o_ref[...] = _layernorm(alpha * x_ref[...] + gate_mod * y, g_ref[...], b_ref[...])


def _peer_dense(x2, mod, layer, seq, ht, c, e1, r2, e2, u_bf, v_bf, g, b, alpha):
    n_tok, d = x2.shape
    n_head, n_keys, _ = c.shape
    n_exp = u_bf.shape[0]
    t = min(n_tok, 512)
    chunk = 2 * n_keys
    n_c = 8
    n_chunk = n_exp // chunk
    n_k = n_chunk // n_c
    vt3 = v_bf.reshape(n_chunk, chunk, d).transpose(0, 2, 1)
    tok3 = pl.BlockSpec((n_head, n_keys, t), lambda i, j: (0, 0, i))
    row = lambda v: v.reshape(1, -1)
    return pl.pallas_call(
        functools.partial(_peer_dense_kernel, n_head=n_head, n_keys=n_keys, alpha=alpha),
        grid=(n_tok // t, n_k),
        in_specs=[pl.BlockSpec((d, t), lambda i, j: (0, i)),
                  pl.BlockSpec((n_c * chunk, d), lambda i, j: (j, 0)),
                  pl.BlockSpec((chunk, d), lambda i, j: (jnp.minimum((j + 1) * n_c, n_chunk - 1), 0)),
                  pl.BlockSpec((n_c, d, chunk), lambda i, j: (j, 0, 0)),
                  pl.BlockSpec((1, d, chunk), lambda i, j: (jnp.maximum(j * n_c - 1, 0), 0, 0)),
                  tok3, tok3, tok3, tok3,
                  pl.BlockSpec((t, d), lambda i, j: (i, 0)),
                  pl.BlockSpec((1, 1, 1, 3 * d), lambda i, j: (layer, (i * t) // seq, 0, 0)),
                  pl.BlockSpec((1, d), lambda i, j: (0, 0)),
                  pl.BlockSpec((1, d), lambda i, j: (0, 0))],
        out_specs=pl.BlockSpec((t, d), lambda i, j: (i, 0)),
        out_shape=jax.ShapeDtypeStruct((n_tok, d), F32),
        scratch_shapes=[pltpu.VMEM((chunk, t), F32), pltpu.VMEM((chunk, t), F32),
                        pltpu.VMEM((chunk, t), BF16), pltpu.VMEM((chunk, t), BF16),
                        pltpu.VMEM((d, t), F32)],
        compiler_params=_cparams(("arbitrary", "arbitrary")),
        name="peer_dense",
    )(ht, u_bf, u_bf, vt3, vt3, c, e1, r2, e2, x2, mod, row(g), row(b))


def _peer_layer(x, mod, layer, wq, k1, k2, u, v, g, b, alpha):
    bn, s, d = x.shape
    n_keys, half_key = k1.shape
    n_head = wq.shape[1] // (2 * half_key)
    x2 = x.reshape(bn * s, d)
    ht, c, e1, r2, e2 = _peer_route(x2, mod, layer, s, wq.T.astype(BF16), k1.astype(BF16),
                                    k2.astype(BF16), n_head)
    out = _peer_dense(x2, mod, layer, s, ht, c, e1, r2, e2, u.astype(BF16), v.astype(BF16),
                      g, b, alpha)
    return out.reshape(bn, s, d)


def kernel(x, c, ada_mix_w, ada_mix_b, ln_mix_g, ln_mix_b, conv_in_w, conv_in_b, conv_dw_w, conv_dw_b, conv_ln_g, conv_ln_b, conv_out_w, conv_out_b, attn_in_w, attn_in_b, attn_out_w, attn_out_b, ada_ffn_w, ada_ffn_b, ln_ffn_g, ln_ffn_b, peer_query_w, peer_sub_keys_1, peer_sub_keys_2, peer_expert_u, peer_expert_v):
    depth = ada_mix_w.shape[0]
    d = x.shape[-1]
    n_attn_head = attn_in_w.shape[-1] - 3 * d
    alpha = (2 * depth) ** 0.25
    mod_mix = _adaln(c, ada_mix_w, ada_mix_b)
    mod_ffn = _adaln(c, ada_ffn_w, ada_ffn_b)
    for i in range(depth):
        j = i // 2
        if i % 2 == 0:
            x = _conv_mixer(x, mod_mix, i, conv_in_w[j], conv_in_b[j], conv_dw_w[j], conv_dw_b[j],
                            conv_ln_g[j], conv_ln_b[j], conv_out_w[j], conv_out_b[j],
                            ln_mix_g[i], ln_mix_b[i], alpha)
        else:
            q, k, v, cum = _qkv(x, mod_mix, i, attn_in_w[j], attn_in_b[j], n_attn_head)
            cum_t = jnp.swapaxes(cum[:, :, :n_attn_head], 1, 2)
            o = _attention(q, k, v, cum_t[..., None], cum_t[:, :, None, :], d // n_attn_head)
            x = _attn_out(o, x, mod_mix, i, attn_out_w[j], attn_out_b[j],
                          ln_mix_g[i], ln_mix_b[i], alpha)
        x = _peer_layer(x, mod_ffn, i, peer_query_w[i], peer_sub_keys_1[i], peer_sub_keys_2[i],
                        peer_expert_u[i], peer_expert_v[i], ln_ffn_g[i], ln_ffn_b[i], alpha)
    return x
```

```python
import functools
import math

import jax
import jax.numpy as jnp
from jax import lax
from jax.experimental import pallas as pl
from jax.experimental.pallas import tpu as pltpu

F32 = jnp.float32
BF16 = jnp.bfloat16

LN_EPS = 1e-5
CONV_HALO = 32
PEER_TOPK = 16
V7X_VMEM_LIMIT_BYTES = 56 * 1024 * 1024
SUBLANES = 8
LANES = 128
INV_SQRT2 = 1.0 / math.sqrt(2.0)
NEG_INF = float("-inf")


def _cparams(sem):
    return pltpu.CompilerParams(dimension_semantics=sem, vmem_limit_bytes=V7X_VMEM_LIMIT_BYTES)


def _layernorm(z, g, b):
    mu = jnp.mean(z, axis=-1, keepdims=True)
    zc = z - mu
    var = jnp.mean(zc * zc, axis=-1, keepdims=True)
    return zc * lax.rsqrt(var + LN_EPS) * g + b


def _split_mod(mod, d):
    return mod[:, :d], mod[:, d:2 * d], mod[:, 2 * d:]


def _ada_kernel(c_ref, w_ref, b_ref, o_ref):
    c = c_ref[...]
    sc = (c * jax.nn.sigmoid(c)).astype(BF16)
    o_ref[0] = jnp.dot(sc, w_ref[0].astype(BF16), preferred_element_type=F32) + b_ref[0]


def _adaln(c, w, b):
    n_layer, d, d3 = w.shape
    bn = c.shape[0]
    tn = min(d3, 1024)
    out = pl.pallas_call(
        _ada_kernel,
        grid=(n_layer, d3 // tn),
        in_specs=[pl.BlockSpec((bn, d), lambda l, j: (0, 0)),
                  pl.BlockSpec((1, d, tn), lambda l, j: (l, 0, j)),
                  pl.BlockSpec((1, 1, tn), lambda l, j: (l, 0, j))],
        out_specs=pl.BlockSpec((1, bn, tn), lambda l, j: (l, 0, j)),
        out_shape=jax.ShapeDtypeStruct((n_layer, bn, d3), F32),
        compiler_params=_cparams(("arbitrary", "arbitrary")),
        name="adaln",
    )(c, w, b.reshape(n_layer, 1, d3))
    return out.reshape(n_layer, bn, 1, d3)


def _conv_kernel(x_ref, mod_ref, w_in_ref, b_in_ref, w_dw_ref, b_dw_ref, cg_ref, cb_ref,
                 w_out_ref, b_out_ref, g_ref, b_ref, o_ref, abuf, shbuf, *, width, alpha):
    t, d = x_ref.shape[1], x_ref.shape[2]

    @pl.when(pl.program_id(1) == 0)
    def _():
        abuf[0:CONV_HALO, :] = jnp.zeros((CONV_HALO, d), F32)

    x = x_ref[0]
    shift, scale, gate = _split_mod(mod_ref[0, 0], d)
    h = (x * (1.0 + scale) + shift).astype(BF16)
    p = jnp.dot(h, w_in_ref[...], preferred_element_type=F32) + b_in_ref[...]
    abuf[CONV_HALO:CONV_HALO + t, :] = p[:, :d] * jax.nn.sigmoid(p[:, d:])
    base = CONV_HALO - (width - 1)
    n_sh = shbuf.shape[1]
    for s in range(1, SUBLANES):
        shbuf[s - 1] = abuf[s:s + n_sh, :]
    acc = jnp.zeros((t, d), F32) + b_dw_ref[...]
    for k in range(width):
        row0, s = (base + k) // SUBLANES * SUBLANES, (base + k) % SUBLANES
        tap = abuf[row0:row0 + t, :] if s == 0 else shbuf[s - 1, row0:row0 + t, :]
        acc = acc + w_dw_ref[k:k + 1, :] * tap
    abuf[0:CONV_HALO, :] = abuf[t:t + CONV_HALO, :]
    a = _layernorm(acc, cg_ref[...], cb_ref[...])
    a = (a * jax.nn.sigmoid(a)).astype(BF16)
    y = jnp.dot(a, w_out_ref[...], preferred_element_type=F32) + b_out_ref[...]
    o_ref[0] = _layernorm(alpha * x + gate * y, g_ref[...], b_ref[...])


def _conv_mixer(x, mod, layer, w_in, b_in, w_dw, b_dw, cg, cb, w_out, b_out, g, b, alpha):
    bn, s, d = x.shape
    t = min(s, 512)
    width = w_dw.shape[0]
    row = lambda v: v.reshape(1, -1)
    const = lambda shape: pl.BlockSpec(shape, lambda i, j: (0,) * len(shape))
    return pl.pallas_call(
        functools.partial(_conv_kernel, width=width, alpha=alpha),
        grid=(bn, s // t),
        in_specs=[pl.BlockSpec((1, t, d), lambda i, j: (i, j, 0)),
                  pl.BlockSpec((1, 1, 1, 3 * d), lambda i, j: (layer, i, 0, 0)),
                  const((d, 2 * d)), const((1, 2 * d)), const((width, d)), const((1, d)),
                  const((1, d)), const((1, d)), const((d, d)), const((1, d)),
                  const((1, d)), const((1, d))],
        out_specs=pl.BlockSpec((1, t, d), lambda i, j: (i, j, 0)),
        out_shape=jax.ShapeDtypeStruct((bn, s, d), F32),
        scratch_shapes=[pltpu.VMEM((t + CONV_HALO, d), F32),
                        pltpu.VMEM((SUBLANES - 1, t + CONV_HALO - SUBLANES, d), F32)],
        compiler_params=_cparams(("arbitrary", "arbitrary")),
        name="conv_mixer",
    )(x, mod, w_in.astype(BF16), row(b_in), w_dw, row(b_dw), row(cg), row(cb),
      w_out.astype(BF16), row(b_out), row(g), row(b))


def _qkv_kernel(x_ref, mod_ref, w_ref, b_ref, wf_ref, bf_ref, q_ref, k_ref, v_ref, cum_ref, carry,
                *, q_scale):
    t, d = x_ref.shape[1], x_ref.shape[2]

    @pl.when(pl.program_id(1) == 0)
    def _():
        carry[...] = jnp.zeros(carry.shape, F32)

    shift, scale, _ = _split_mod(mod_ref[0, 0], d)
    h = (x_ref[0] * (1.0 + scale) + shift).astype(BF16)
    proj = jnp.dot(h, w_ref[...], preferred_element_type=F32) + b_ref[...]
    q_ref[0] = (proj[:, :d] * q_scale).astype(BF16)
    k_ref[0] = proj[:, d:2 * d].astype(BF16)
    v_ref[0] = proj[:, 2 * d:].astype(BF16)
    f = jnp.dot(h, wf_ref[...], preferred_element_type=F32) + bf_ref[...]
    log_f = jax.nn.log_sigmoid(f)
    r = lax.broadcasted_iota(jnp.int32, (t, t), 0)
    c = lax.broadcasted_iota(jnp.int32, (t, t), 1)
    tri = jnp.where(c <= r, 1.0, 0.0).astype(BF16)
    hi = log_f.astype(BF16)
    rem = log_f - hi.astype(F32)
    mid = rem.astype(BF16)
    lo = (rem - mid.astype(F32)).astype(BF16)
    cum = (jnp.dot(tri, hi, preferred_element_type=F32)
           + jnp.dot(tri, mid, preferred_element_type=F32)
           + jnp.dot(tri, lo, preferred_element_type=F32)) + carry[0:1, :]
    cum_ref[0] = cum
    carry[0:1, :] = cum[t - 1:t, :]


def _qkv(x, mod, layer, w_in, b_in, n_head):
    bn, s, d = x.shape
    hd = d // n_head
    t = min(s, 512)
    w_qkv = w_in[:, :3 * d].astype(BF16)
    b_qkv = b_in[:3 * d].reshape(1, 3 * d)
    w_f = jnp.pad(w_in[:, 3 * d:], ((0, 0), (0, LANES - n_head))).astype(BF16)
    b_f = jnp.pad(b_in[3 * d:], (0, LANES - n_head)).reshape(1, LANES)
    const = lambda shape: pl.BlockSpec(shape, lambda i, j: (0,) * len(shape))
    head_spec = pl.BlockSpec((1, t, d), lambda i, j: (i, j, 0))
    head_shape = jax.ShapeDtypeStruct((bn, s, d), BF16)
    return pl.pallas_call(
        functools.partial(_qkv_kernel, q_scale=hd ** -0.5),
        grid=(bn, s // t),
        in_specs=[pl.BlockSpec((1, t, d), lambda i, j: (i, j, 0)),
                  pl.BlockSpec((1, 1, 1, 3 * d), lambda i, j: (layer, i, 0, 0)),
                  const((d, 3 * d)), const((1, 3 * d)), const((d, LANES)), const((1, LANES))],
        out_specs=[head_spec, head_spec, head_spec,
                   pl.BlockSpec((1, t, LANES), lambda i, j: (i, j, 0))],
        out_shape=[head_shape, head_shape, head_shape,
                   jax.ShapeDtypeStruct((bn, s, LANES), F32)],
        scratch_shapes=[pltpu.VMEM((SUBLANES, LANES), F32)],
        compiler_params=_cparams(("arbitrary", "arbitrary")),
        name="fox_qkv",
    )(x, mod, w_qkv, b_qkv, w_f, b_f)


def _attn_kernel(q_ref, k_ref, v_ref, fq_ref, fk_ref, o_ref, vt_aug, *, tk, hd):
    tq = q_ref.shape[1]
    n_sub = LANES // hd
    qi = pl.program_id(2)

    @pl.when(qi == 0)
    def _():
        row = lax.broadcasted_iota(jnp.int32, (LANES, tk), 0)
        ones_row = jnp.where(row == 0, 1.0, 0.0).astype(vt_aug.dtype)
        for kv in range(vt_aug.shape[0]):
            v_blk = v_ref[0, kv * tk:(kv + 1) * tk, :]
            vt_aug[kv, 0:LANES, :] = v_blk.astype(F32).T.astype(vt_aug.dtype)
            vt_aug[kv, LANES:2 * LANES, :] = ones_row

    q2 = q_ref[0]
    lane = lax.broadcasted_iota(jnp.int32, (tq, LANES), 1)
    qh = [jnp.where((lane >= j * hd) & (lane < (j + 1) * hd), q2, jnp.zeros((), q2.dtype))
          for j in range(n_sub)]
    fq = [fq_ref[0, j] for j in range(n_sub)]

    def step(kv, carry, masked):
        off = pl.multiple_of(kv * tk, tk)
        k = k_ref[0, pl.ds(off, tk), :]
        vt = vt_aug[kv]
        out = []
        for j in range(n_sub):
            m, acc = carry[j]
            z = lax.dot_general(k, qh[j], (((1,), (1,)), ((), ())), preferred_element_type=F32)
            z = z - fk_ref[0, j, pl.ds(off, tk), :]
            if masked:
                r = lax.broadcasted_iota(jnp.int32, (tk, tq), 0)
                c = lax.broadcasted_iota(jnp.int32, (tk, tq), 1)
                z = jnp.where(r <= c, z, NEG_INF)
            m_new = jnp.maximum(m, jnp.max(z, axis=0, keepdims=True) + fq[j])
            a = jnp.exp(m - m_new)
            p = jnp.exp((z - (m_new - fq[j])).astype(BF16))
            acc = a * acc + jnp.dot(vt, p, preferred_element_type=F32)
            out.append((m_new, acc))
        return tuple(out)

    init = tuple((jnp.full((1, tq), NEG_INF, F32), jnp.zeros((2 * LANES, tq), F32))
                 for _ in range(n_sub))
    carry = lax.fori_loop(0, qi, lambda kv, c: step(kv, c, False), init)
    res = step(qi, carry, True)
    o = None
    for j in range(n_sub):
        acc = res[j][1]
        o_j = (acc[:LANES, :] / acc[LANES:LANES + 1, :]).T
        o = o_j if o is None else jnp.where(lane >= j * hd, o_j, o)
    o_ref[0] = o.astype(o_ref.dtype)


def _attention(q, k, v, fq, fk, hd):
    bn, s, d = q.shape
    n_sub = LANES // hd
    t = min(s, 512)
    return pl.pallas_call(
        functools.partial(_attn_kernel, tk=t, hd=hd),
        grid=(bn, d // LANES, s // t),
        in_specs=[pl.BlockSpec((1, t, LANES), lambda b, h, i: (b, i, h)),
                  pl.BlockSpec((1, s, LANES), lambda b, h, i: (b, 0, h)),
                  pl.BlockSpec((1, s, LANES), lambda b, h, i: (b, 0, h)),
                  pl.BlockSpec((1, n_sub, 1, t), lambda b, h, i: (b, h, 0, i)),
                  pl.BlockSpec((1, n_sub, s, 1), lambda b, h, i: (b, h, 0, 0))],
        out_specs=pl.BlockSpec((1, t, LANES), lambda b, h, i: (b, i, h)),
        out_shape=jax.ShapeDtypeStruct((bn, s, d), BF16),
        scratch_shapes=[pltpu.VMEM((s // t, 2 * LANES, t), BF16)],
        compiler_params=_cparams(("arbitrary", "arbitrary", "arbitrary")),
        name="fox_attention",
    )(q, k, v, fq, fk)


def _attn_out_kernel(o_ref, x_ref, mod_ref, w_ref, b_ref, g_ref, bb_ref, out_ref, *, alpha):
    d = x_ref.shape[2]
    y = jnp.dot(o_ref[0], w_ref[...], preferred_element_type=F32) + b_ref[...]
    _, _, gate = _split_mod(mod_ref[0, 0], d)
    out_ref[0] = _layernorm(alpha * x_ref[0] + gate * y, g_ref[...], bb_ref[...])


def _attn_out(o, x, mod, layer, w_out, b_out, g, b, alpha):
    bn, s, d = x.shape
    t = min(s, 512)
    row = lambda v: v.reshape(1, -1)
    const = lambda shape: pl.BlockSpec(shape, lambda i, j: (0,) * len(shape))
    return pl.pallas_call(
        functools.partial(_attn_out_kernel, alpha=alpha),
        grid=(bn, s // t),
        in_specs=[pl.BlockSpec((1, t, d), lambda i, j: (i, j, 0)),
                  pl.BlockSpec((1, t, d), lambda i, j: (i, j, 0)),
                  pl.BlockSpec((1, 1, 1, 3 * d), lambda i, j: (layer, i, 0, 0)),
                  const((d, d)), const((1, d)), const((1, d)), const((1, d))],
        out_specs=pl.BlockSpec((1, t, d), lambda i, j: (i, j, 0)),
        out_shape=jax.ShapeDtypeStruct((bn, s, d), F32),
        compiler_params=_cparams(("arbitrary", "arbitrary")),
        name="fox_out",
    )(o, x, mod, w_out.astype(BF16), row(b_out), row(g), row(b))


def _sort_desc(a):
    a = list(a)
    n = len(a)
    k = 2
    while k <= n:
        j = k // 2
        while j >= 1:
            for i in range(n):
                l = i ^ j
                if l > i:
                    hi, lo = jnp.maximum(a[i], a[l]), jnp.minimum(a[i], a[l])
                    a[i], a[l] = (hi, lo) if (i & k) == 0 else (lo, hi)
            j //= 2
        k *= 2
    return a


def _bitonic_merge_desc(t):
    t = list(t)
    n = len(t)
    j = n // 2
    while j >= 1:
        for i in range(n):
            l = i ^ j
            if l > i:
                t[i], t[l] = jnp.maximum(t[i], t[l]), jnp.minimum(t[i], t[l])
        j //= 2
    return t


def _merge_top(a, b):
    n = len(a)
    t = [a[i] if b[n - 1 - i] is None else jnp.maximum(a[i], b[n - 1 - i]) for i in range(n)]
    return _bitonic_merge_desc(t)


def _select_by_bits(table, bits):
    for b in bits:
        table = [jnp.where(b, table[2 * i + 1], table[2 * i]) for i in range(len(table) // 2)]
    return table[0]


def _rank_bits(v, s):
    n = len(v)
    known = []
    step = n // 2
    while step >= 1:
        pivots = [v[lo + step - 1] for lo in range(0, n, 2 * step)]
        known.append(_select_by_bits(pivots, known[::-1]) > s)
        step //= 2
    return known[::-1], v[n - 1] > s


def _top_values(s_ref, tk):
    n_keys = s_ref.shape[0]
    groups = [s_ref[g * SUBLANES:(g + 1) * SUBLANES, :] for g in range(n_keys // SUBLANES)]
    top = _sort_desc(groups[:tk])
    for g0 in range(tk, len(groups), tk):
        top = _merge_top(top, _sort_desc(groups[g0:g0 + tk]))
    shift = SUBLANES // 2
    while shift >= 1:
        top = _merge_top(top, [pltpu.roll(v, shift, 0) for v in top])
        shift //= 2
    return top


def _candidate_counts(v1, v2, tk):
    cand = {}
    for i in range(tk):
        for j in range(tk // (i + 1)):
            cand[i, j] = v1[i] + v2[j]
    half = tk // 2
    top = [cand[0, j] for j in range(tk)]
    lst_a = [cand[1, j] for j in range(half)] + [cand[i, 0] for i in range(tk - 1, half - 1, -1)]
    top = _merge_top(top, _bitonic_merge_desc(lst_a))
    rest = [cand[i, j] for i in range(2, half) for j in range(tk // (i + 1))]
    for g0 in range(0, len(rest), tk):
        grp = rest[g0:g0 + tk]
        if len(grp) == tk:
            grp = _sort_desc(grp)
        else:
            grp = _sort_desc(grp + [jnp.full_like(grp[0], NEG_INF)] * (tk - len(grp)))
        top = _merge_top(top, grp)
    tau = top[tk - 1]
    e1 = [jnp.exp(v1[i] - v1[0]) for i in range(tk)]
    e2 = [jnp.exp(v2[j] - v2[0]) for j in range(tk)]
    cnt, z = [], None
    for i in range(tk):
        c_i, z_i = None, None
        for j in range(tk // (i + 1)):
            sel = cand[i, j] >= tau
            one = jnp.where(sel, 1.0, 0.0)
            w = jnp.where(sel, e2[j], 0.0)
            c_i = one if c_i is None else c_i + one
            z_i = w if z_i is None else z_i + w
        cnt.append(c_i)
        z = e1[i] * z_i if z is None else z + e1[i] * z_i
    return cnt, 1.0 / z


def _route_kernel(x_ref, mod_ref, wq_ref, k1_ref, k2_ref,
                  ht_ref, c_ref, e1_ref, r2_ref, e2_ref,
                  q_scr, s_scr, v_scr, vh_scr, cs_scr, *, n_head, tk):
    t, d = x_ref.shape
    n_keys = k1_ref.shape[0]
    shift, scale, _ = _split_mod(mod_ref[0, 0], d)
    h = x_ref[...] * (1.0 + scale) + shift
    ht = h.T.astype(BF16)
    ht_ref[...] = ht
    q_scr[...] = jnp.dot(wq_ref[...], ht, preferred_element_type=F32).astype(BF16)

    def scores_and_top(hh, carry):
        for half, k_ref in enumerate((k1_ref, k2_ref)):
            off = pl.multiple_of((2 * hh + half) * n_keys, n_keys)
            s_scr[hh, half] = jnp.dot(k_ref[...], q_scr[pl.ds(off, n_keys), :],
                                      preferred_element_type=F32)
            top = _top_values(s_scr.at[hh, half], tk)
            for i in range(tk):
                v_scr[hh, half, i] = top[i]
                vh_scr[half, i, pl.ds(hh, 1), :] = top[i][0:1, :]
        return carry

    lax.fori_loop(0, n_head, scores_and_top, 0)

    v1 = [vh_scr[0, i] for i in range(tk)]
    v2 = [vh_scr[1, i] for i in range(tk)]
    cnt, inv_z = _candidate_counts(v1, v2, tk)
    for i in range(tk):
        cs_scr[i] = cnt[i]
    cs_scr[tk] = inv_z

    def dense(hh, carry):
        def row(idx):
            return jnp.broadcast_to(cs_scr[idx, pl.ds(hh, 1), :], (SUBLANES, t))
        cnt_h = [row(i) for i in range(tk)]
        inv = row(tk)
        v1h = [v_scr[hh, 0, i] for i in range(tk)]
        v2h = [v_scr[hh, 1, i] for i in range(tk)]
        for g in range(n_keys // SUBLANES):
            rows = slice(g * SUBLANES, (g + 1) * SUBLANES)
            s1 = s_scr[hh, 0, rows, :]
            s2 = s_scr[hh, 1, rows, :]
            bits1, below1 = _rank_bits(v1h, s1)
            bits2, below2 = _rank_bits(v2h, s2)
            c = jnp.where(below1, 0.0, _select_by_bits(cnt_h, bits1))
            r2 = jnp.where(below2, 1.0, 0.0)
            for b, weight in zip(bits2, (1.0, 2.0, 4.0, 8.0)):
                r2 = r2 + jnp.where(b, weight, 0.0)
            c_ref[hh, rows, :] = c
            e1_ref[hh, rows, :] = jnp.exp(s1 - v1h[0])
            r2_ref[hh, rows, :] = r2.astype(BF16)
            e2_ref[hh, rows, :] = (jnp.exp(s2 - v2h[0]) * inv).astype(BF16)
        return carry

    lax.fori_loop(0, n_head, dense, 0)


def _peer_route(x2, mod, layer, seq, wq_t, k1, k2, n_head):
    n_tok, d = x2.shape
    n_keys = k1.shape[0]
    t = min(n_tok, 512)
    tk = PEER_TOPK
    const = lambda shape: pl.BlockSpec(shape, lambda i: (0,) * len(shape))
    tok3 = pl.BlockSpec((n_head, n_keys, t), lambda i: (0, 0, i))
    return pl.pallas_call(
        functools.partial(_route_kernel, n_head=n_head, tk=tk),
        grid=(n_tok // t,),
        in_specs=[pl.BlockSpec((t, d), lambda i: (i, 0)),
                  pl.BlockSpec((1, 1, 1, 3 * d), lambda i: (layer, (i * t) // seq, 0, 0)),
                  const(wq_t.shape), const(k1.shape), const(k2.shape)],
        out_specs=[pl.BlockSpec((d, t), lambda i: (0, i)), tok3, tok3, tok3, tok3],
        out_shape=[jax.ShapeDtypeStruct((d, n_tok), BF16),
                   jax.ShapeDtypeStruct((n_head, n_keys, n_tok), F32),
                   jax.ShapeDtypeStruct((n_head, n_keys, n_tok), F32),
                   jax.ShapeDtypeStruct((n_head, n_keys, n_tok), BF16),
                   jax.ShapeDtypeStruct((n_head, n_keys, n_tok), BF16)],
        scratch_shapes=[pltpu.VMEM((2 * n_head * n_keys, t), BF16),
                        pltpu.VMEM((n_head, 2, n_keys, t), F32),
                        pltpu.VMEM((n_head, 2, tk, SUBLANES, t), F32),
                        pltpu.VMEM((2, tk, SUBLANES, t), F32),
                        pltpu.VMEM((tk + 1, SUBLANES, t), F32)],
        compiler_params=_cparams(("arbitrary",)),
        name="peer_route",
    )(x2, mod, wq_t, k1, k2)


def _peer_dense_kernel(ht_ref, u_ref, un_ref, vt_ref, vp_ref, c_ref, e1_ref, r2_ref, e2_ref,
                       x_ref, mod_ref, g_ref, b_ref, o_ref, a0, a1, w0, w1, acc_scr,
                       *, n_head, n_keys, alpha):
    t, d = x_ref.shape
    chunk = a0.shape[0]
    n_c = u_ref.shape[0] // chunk
    n_a = chunk // n_keys
    pack = 2 * SUBLANES
    k = pl.program_id(1)

    def u_matmul(u_rows, dst):
        dst[...] = jnp.dot(u_rows, ht_ref[...], preferred_element_type=F32)

    def v_matmul(vt_chunk, w_scr):
        acc_scr[...] += jnp.dot(vt_chunk, w_scr[...], preferred_element_type=F32)

    @pl.when(k == 0)
    def _():
        w1[...] = jnp.zeros(w1.shape, w1.dtype)
        acc_scr[...] = jnp.zeros(acc_scr.shape, F32)
        u_matmul(u_ref[0:chunk, :], a0)

    def gate_pass(a_scr, w_scr, c):
        for a in range(n_a):
            ag = (k * n_c + c) * n_a + a
            pre = a_scr[a * n_keys:(a + 1) * n_keys, :]
            act = (0.5 * pre * (1.0 + lax.erf(pre * INV_SQRT2))).astype(BF16)
            gate = [jnp.zeros((pack, t), BF16) for _ in range(n_keys // pack)]
            for hh in range(n_head):
                cnt = jnp.broadcast_to(c_ref[hh, pl.ds(ag, 1), :], (pack, t)).astype(BF16)
                e1 = jnp.broadcast_to(e1_ref[hh, pl.ds(ag, 1), :], (pack, t)).astype(BF16)
                for bc in range(n_keys // pack):
                    rows = slice(bc * pack, (bc + 1) * pack)
                    sel = jnp.where(r2_ref[hh, rows, :] < cnt, e2_ref[hh, rows, :],
                                    jnp.zeros((), BF16))
                    gate[bc] = gate[bc] + sel * e1
            for bc in range(n_keys // pack):
                r0 = a * n_keys + bc * pack
                w_scr[r0:r0 + pack, :] = act[bc * pack:(bc + 1) * pack, :] * gate[bc]

    def u_rows(c):
        return u_ref[pl.ds(pl.multiple_of(c * chunk, chunk), chunk), :]

    def stage_pair(c, first, last):
        u_matmul(u_rows(c + 1), a1)
        gate_pass(a0, w0, c)
        v_matmul(vp_ref[0] if first else vt_ref[c - 1], w1)
        u_matmul(un_ref[...] if last else u_rows(c + 2), a0)
        gate_pass(a1, w1, c + 1)
        v_matmul(vt_ref[c], w0)

    stage_pair(0, True, n_c == 2)
    if n_c > 4:
        def body(p, carry):
            stage_pair(2 * p, False, False)
            return carry
        lax.fori_loop(1, n_c // 2 - 1, body, 0)
    if n_c > 2:
        stage_pair(n_c - 2, False, True)

    @pl.when(k == pl.num_programs(1) - 1)
    def _():
        v_matmul(vt_ref[n_c - 1], w1)
        _, _, gate_mod = _split_mod(mod_ref[0, 0], d)
        y = acc_scr[...].T
        o_ref[...] = _layernorm(alpha * x_ref[...] + gate_mod * y, g_ref[...], b_ref[...])


def _peer_dense(x2, mod, layer, seq, ht, c, e1, r2, e2, u_bf, v_bf, g, b, alpha):
    n_tok, d = x2.shape
    n_head, n_keys, _ = c.shape
    n_exp = u_bf.shape[0]
    t = min(n_tok, 512)
    chunk = 2 * n_keys
    n_c = 8
    n_chunk = n_exp // chunk
    n_k = n_chunk // n_c
    vt3 = v_bf.reshape(n_chunk, chunk, d).transpose(0, 2, 1)
    tok3 = pl.BlockSpec((n_head, n_keys, t), lambda i, j: (0, 0, i))
    row = lambda v: v.reshape(1, -1)
    return pl.pallas_call(
        functools.partial(_peer_dense_kernel, n_head=n_head, n_keys=n_keys, alpha=alpha),
        grid=(n_tok // t, n_k),
        in_specs=[pl.BlockSpec((d, t), lambda i, j: (0, i)),
                  pl.BlockSpec((n_c * chunk, d), lambda i, j: (j, 0)),
                  pl.BlockSpec((chunk, d), lambda i, j: (jnp.minimum((j + 1) * n_c, n_chunk - 1), 0)),
                  pl.BlockSpec((n_c, d, chunk), lambda i, j: (j, 0, 0)),
                  pl.BlockSpec((1, d, chunk), lambda i, j: (jnp.maximum(j * n_c - 1, 0), 0, 0)),
                  tok3, tok3, tok3, tok3,
                  pl.BlockSpec((t, d), lambda i, j: (i, 0)),
                  pl.BlockSpec((1, 1, 1, 3 * d), lambda i, j: (layer, (i * t) // seq, 0, 0)),
                  pl.BlockSpec((1, d), lambda i, j: (0, 0)),
                  pl.BlockSpec((1, d), lambda i, j: (0, 0))],
        out_specs=pl.BlockSpec((t, d), lambda i, j: (i, 0)),
        out_shape=jax.ShapeDtypeStruct((n_tok, d), F32),
        scratch_shapes=[pltpu.VMEM((chunk, t), F32), pltpu.VMEM((chunk, t), F32),
                        pltpu.VMEM((chunk, t), BF16), pltpu.VMEM((chunk, t), BF16),
                        pltpu.VMEM((d, t), F32)],
        compiler_params=_cparams(("arbitrary", "arbitrary")),
        name="peer_dense",
    )(ht, u_bf, u_bf, vt3, vt3, c, e1, r2, e2, x2, mod, row(g), row(b))


def _peer_layer(x, mod, layer, wq, k1, k2, u, v, g, b, alpha):
    bn, s, d = x.shape
    n_keys, half_key = k1.shape
    n_head = wq.shape[1] // (2 * half_key)
    x2 = x.reshape(bn * s, d)
    ht, c, e1, r2, e2 = _peer_route(x2, mod, layer, s, wq.T.astype(BF16), k1.astype(BF16),
                                    k2.astype(BF16), n_head)
    out = _peer_dense(x2, mod, layer, s, ht, c, e1, r2, e2, u.astype(BF16), v.astype(BF16),
                      g, b, alpha)
    return out.reshape(bn, s, d)


def kernel(x, c, ada_mix_w, ada_mix_b, ln_mix_g, ln_mix_b, conv_in_w, conv_in_b, conv_dw_w, conv_dw_b, conv_ln_g, conv_ln_b, conv_out_w, conv_out_b, attn_in_w, attn_in_b, attn_out_w, attn_out_b, ada_ffn_w, ada_ffn_b, ln_ffn_g, ln_ffn_b, peer_query_w, peer_sub_keys_1, peer_sub_keys_2, peer_expert_u, peer_expert_v):
    depth = ada_mix_w.shape[0]
    d = x.shape[-1]
    n_attn_head = attn_in_w.shape[-1] - 3 * d
    alpha = (2 * depth) ** 0.25
    mod_mix = _adaln(c, ada_mix_w, ada_mix_b)
    mod_ffn = _adaln(c, ada_ffn_w, ada_ffn_b)
    for i in range(depth):
        j = i // 2
        if i % 2 == 0:
            x = _conv_mixer(x, mod_mix, i, conv_in_w[j], conv_in_b[j], conv_dw_w[j], conv_dw_b[j],
                            conv_ln_g[j], conv_ln_b[j], conv_out_w[j], conv_out_b[j],
                            ln_mix_g[i], ln_mix_b[i], alpha)
        else:
            q, k, v, cum = _qkv(x, mod_mix, i, attn_in_w[j], attn_in_b[j], n_attn_head)
            cum_t = jnp.swapaxes(cum[:, :, :n_attn_head], 1, 2)
            o = _attention(q, k, v, cum_t[:, :, None, :], cum_t[..., None], d // n_attn_head)
            x = _attn_out(o, x, mod_mix, i, attn_out_w[j], attn_out_b[j],
                          ln_mix_g[i], ln_mix_b[i], alpha)
        x = _peer_layer(x, mod_ffn, i, peer_query_w[i], peer_sub_keys_1[i], peer_sub_keys_2[i],
                        peer_expert_u[i], peer_expert_v[i], ln_ffn_g[i], ln_ffn_b[i], alpha)
    return x
```

```python
import functools
import math

import jax
import jax.numpy as jnp
from jax import lax
from jax.experimental import pallas as pl
from jax.experimental.pallas import tpu as pltpu

F32 = jnp.float32
BF16 = jnp.bfloat16

LN_EPS = 1e-5
CONV_HALO = 32
PEER_TOPK = 16
V7X_VMEM_LIMIT_BYTES = 56 * 1024 * 1024
SUBLANES = 8
LANES = 128
INV_SQRT2 = 1.0 / math.sqrt(2.0)
NEG_INF = float("-inf")


def _cparams(sem):
    return pltpu.CompilerParams(dimension_semantics=sem, vmem_limit_bytes=V7X_VMEM_LIMIT_BYTES)


def _layernorm(z, g, b):
    mu = jnp.mean(z, axis=-1, keepdims=True)
    zc = z - mu
    var = jnp.mean(zc * zc, axis=-1, keepdims=True)
    return zc * lax.rsqrt(var + LN_EPS) * g + b


def _split_mod(mod, d):
    return mod[:, :d], mod[:, d:2 * d], mod[:, 2 * d:]


def _ada_kernel(c_ref, w_ref, b_ref, o_ref):
    c = c_ref[...]
    sc = (c * jax.nn.sigmoid(c)).astype(BF16)
    o_ref[0] = jnp.dot(sc, w_ref[0].astype(BF16), preferred_element_type=F32) + b_ref[0]


def _adaln(c, w, b):
    n_layer, d, d3 = w.shape
    bn = c.shape[0]
    tn = min(d3, 1024)
    out = pl.pallas_call(
        _ada_kernel,
        grid=(n_layer, d3 // tn),
        in_specs=[pl.BlockSpec((bn, d), lambda l, j: (0, 0)),
                  pl.BlockSpec((1, d, tn), lambda l, j: (l, 0, j)),
                  pl.BlockSpec((1, 1, tn), lambda l, j: (l, 0, j))],
        out_specs=pl.BlockSpec((1, bn, tn), lambda l, j: (l, 0, j)),
        out_shape=jax.ShapeDtypeStruct((n_layer, bn, d3), F32),
        compiler_params=_cparams(("arbitrary", "arbitrary")),
        name="adaln",
    )(c, w, b.reshape(n_layer, 1, d3))
    return out.reshape(n_layer, bn, 1, d3)


def _conv_kernel(x_ref, mod_ref, w_in_ref, b_in_ref, w_dw_ref, b_dw_ref, cg_ref, cb_ref,
                 w_out_ref, b_out_ref, g_ref, b_ref, o_ref, abuf, shbuf, *, width, alpha):
    t, d = x_ref.shape[1], x_ref.shape[2]

    @pl.when(pl.program_id(1) == 0)
    def _():
        abuf[0:CONV_HALO, :] = jnp.zeros((CONV_HALO, d), F32)

    x = x_ref[0]
    shift, scale, gate = _split_mod(mod_ref[0, 0], d)
    h = (x * (1.0 + scale) + shift).astype(BF16)
    p = jnp.dot(h, w_in_ref[...], preferred_element_type=F32) + b_in_ref[...]
    abuf[CONV_HALO:CONV_HALO + t, :] = p[:, :d] * jax.nn.sigmoid(p[:, d:])
    base = CONV_HALO - (width - 1)
    n_sh = shbuf.shape[1]
    for s in range(1, SUBLANES):
        shbuf[s - 1] = abuf[s:s + n_sh, :]
    acc = jnp.zeros((t, d), F32) + b_dw_ref[...]
    for k in range(width):
        row0, s = (base + k) // SUBLANES * SUBLANES, (base + k) % SUBLANES
        tap = abuf[row0:row0 + t, :] if s == 0 else shbuf[s - 1, row0:row0 + t, :]
        acc = acc + w_dw_ref[k:k + 1, :] * tap
    abuf[0:CONV_HALO, :] = abuf[t:t + CONV_HALO, :]
    a = _layernorm(acc, cg_ref[...], cb_ref[...])
    a = (a * jax.nn.sigmoid(a)).astype(BF16)
    y = jnp.dot(a, w_out_ref[...], preferred_element_type=F32) + b_out_ref[...]
    o_ref[0] = _layernorm(alpha * x + gate * y, g_ref[...], b_ref[...])


def _conv_mixer(x, mod, layer, w_in, b_in, w_dw, b_dw, cg, cb, w_out, b_out, g, b, alpha):
    bn, s, d = x.shape
    t = min(s, 512)
    width = w_dw.shape[0]
    row = lambda v: v.reshape(1, -1)
    const = lambda shape: pl.BlockSpec(shape, lambda i, j: (0,) * len(shape))
    return pl.pallas_call(
        functools.partial(_conv_kernel, width=width, alpha=alpha),
        grid=(bn, s // t),
        in_specs=[pl.BlockSpec((1, t, d), lambda i, j: (i, j, 0)),
                  pl.BlockSpec((1, 1, 1, 3 * d), lambda i, j: (layer, i, 0, 0)),
                  const((d, 2 * d)), const((1, 2 * d)), const((width, d)), const((1, d)),
                  const((1, d)), const((1, d)), const((d, d)), const((1, d)),
                  const((1, d)), const((1, d))],
        out_specs=pl.BlockSpec((1, t, d), lambda i, j: (i, j, 0)),
        out_shape=jax.ShapeDtypeStruct((bn, s, d), F32),
        scratch_shapes=[pltpu.VMEM((t + CONV_HALO, d), F32),
                        pltpu.VMEM((SUBLANES - 1, t + CONV_HALO - SUBLANES, d), F32)],
        compiler_params=_cparams(("arbitrary", "arbitrary")),
        name="conv_mixer",
    )(x, mod, w_in.astype(BF16), row(b_in), w_dw, row(b_dw), row(cg), row(cb),
      w_out.astype(BF16), row(b_out), row(g), row(b))


def _qkv_kernel(x_ref, mod_ref, w_ref, b_ref, wf_ref, bf_ref, q_ref, k_ref, v_ref, cum_ref, carry,
                *, q_scale):
    t, d = x_ref.shape[1], x_ref.shape[2]

    @pl.when(pl.program_id(1) == 0)
    def _():
        carry[...] = jnp.zeros(carry.shape, F32)

    shift, scale, _ = _split_mod(mod_ref[0, 0], d)
    h = (x_ref[0] * (1.0 + scale) + shift).astype(BF16)
    proj = jnp.dot(h, w_ref[...], preferred_element_type=F32) + b_ref[...]
    q_ref[0] = (proj[:, :d] * q_scale).astype(BF16)
    k_ref[0] = proj[:, d:2 * d].astype(BF16)
    v_ref[0] = proj[:, 2 * d:].astype(BF16)
    f = jnp.dot(h, wf_ref[...], preferred_element_type=F32) + bf_ref[...]
    log_f = jax.nn.log_sigmoid(f)
    r = lax.broadcasted_iota(jnp.int32, (t, t), 0)
    c = lax.broadcasted_iota(jnp.int32, (t, t), 1)
    tri = jnp.where(c <= r, 1.0, 0.0).astype(BF16)
    hi = log_f.astype(BF16)
    rem = log_f - hi.astype(F32)
    mid = rem.astype(BF16)
    lo = (rem - mid.astype(F32)).astype(BF16)
    cum = (jnp.dot(tri, hi, preferred_element_type=F32)
           + jnp.dot(tri, mid, preferred_element_type=F32)
           + jnp.dot(tri, lo, preferred_element_type=F32)) + carry[0:1, :]
    cum_ref[0] = cum
    carry[0:1, :] = cum[t - 1:t, :]


def _qkv(x, mod, layer, w_in, b_in, n_head):
    bn, s, d = x.shape
    hd = d // n_head
    t = min(s, 512)
    w_qkv = w_in[:, :3 * d].astype(BF16)
    b_qkv = b_in[:3 * d].reshape(1, 3 * d)
    w_f = jnp.pad(w_in[:, 3 * d:], ((0, 0), (0, LANES - n_head))).astype(BF16)
    b_f = jnp.pad(b_in[3 * d:], (0, LANES - n_head)).reshape(1, LANES)
    const = lambda shape: pl.BlockSpec(shape, lambda i, j: (0,) * len(shape))
    head_spec = pl.BlockSpec((1, t, d), lambda i, j: (i, j, 0))
    head_shape = jax.ShapeDtypeStruct((bn, s, d), BF16)
    return pl.pallas_call(
        functools.partial(_qkv_kernel, q_scale=hd ** -0.5),
        grid=(bn, s // t),
        in_specs=[pl.BlockSpec((1, t, d), lambda i, j: (i, j, 0)),
                  pl.BlockSpec((1, 1, 1, 3 * d), lambda i, j: (layer, i, 0, 0)),
                  const((d, 3 * d)), const((1, 3 * d)), const((d, LANES)), const((1, LANES))],
        out_specs=[head_spec, head_spec, head_spec,
                   pl.BlockSpec((1, t, LANES), lambda i, j: (i, j, 0))],
        out_shape=[head_shape, head_shape, head_shape,
                   jax.ShapeDtypeStruct((bn, s, LANES), F32)],
        scratch_shapes=[pltpu.VMEM((SUBLANES, LANES), F32)],
        compiler_params=_cparams(("arbitrary", "arbitrary")),
        name="fox_qkv",
    )(x, mod, w_qkv, b_qkv, w_f, b_f)


def _attn_kernel(q_ref, k_ref, v_ref, fq_ref, fk_ref, o_ref, vt_aug, *, tk, hd):
    tq = q_ref.shape[1]
    n_sub = LANES // hd
    n_aug = vt_aug.shape[2]
    qi = pl.program_id(2)

    @pl.when(qi == 0)
    def _():
        row = lax.broadcasted_iota(jnp.int32, (n_aug - hd, tk), 0)
        ones_row = jnp.where(row == 0, 1.0, 0.0).astype(vt_aug.dtype)
        for kv in range(vt_aug.shape[0]):
            v_t = v_ref[0, kv * tk:(kv + 1) * tk, :].astype(F32).T.astype(vt_aug.dtype)
            for j in range(n_sub):
                vt_aug[kv, j, 0:hd, :] = v_t[j * hd:(j + 1) * hd, :]
                vt_aug[kv, j, hd:n_aug, :] = ones_row

    q2 = q_ref[0]
    lane = lax.broadcasted_iota(jnp.int32, (tq, LANES), 1)
    qh = [jnp.where((lane >= j * hd) & (lane < (j + 1) * hd), q2, jnp.zeros((), q2.dtype))
          for j in range(n_sub)]
    fq = [fq_ref[0, j] for j in range(n_sub)]

    def step(kv, carry, masked):
        off = pl.multiple_of(kv * tk, tk)
        k = k_ref[0, pl.ds(off, tk), :]
        out = []
        for j in range(n_sub):
            m, acc = carry[j]
            z = lax.dot_general(k, qh[j], (((1,), (1,)), ((), ())), preferred_element_type=F32)
            z = z - fk_ref[0, j, pl.ds(off, tk), :]
            if masked:
                key = off + lax.broadcasted_iota(jnp.int32, (tk, tq), 0)
                qry = qi * tq + lax.broadcasted_iota(jnp.int32, (tk, tq), 1)
                z = jnp.where(key <= qry, z, NEG_INF)
            m_new = jnp.maximum(m, jnp.max(z, axis=0, keepdims=True) + fq[j])
            a = jnp.exp(m - m_new)
            p = jnp.exp((z - (m_new - fq[j])).astype(BF16))
            acc = a * acc + jnp.dot(vt_aug[kv, j], p, preferred_element_type=F32)
            out.append((m_new, acc))
        return tuple(out)

    n_diag = tq // tk
    carry = tuple((jnp.full((1, tq), NEG_INF, F32), jnp.zeros((n_aug, tq), F32))
                  for _ in range(n_sub))
    carry = lax.fori_loop(0, qi * n_diag, lambda kv, c: step(kv, c, False), carry)
    for i in range(n_diag):
        carry = step(qi * n_diag + i, carry, True)
    res = carry
    o_t = jnp.concatenate([acc[:hd, :] / acc[hd:hd + 1, :] for _, acc in res], axis=0)
    o_ref[0] = o_t.T.astype(o_ref.dtype)


def _attention(q, k, v, fq, fk, hd):
    bn, s, d = q.shape
    n_sub = LANES // hd
    tk = min(s, 512)
    tq = min(s, 1024)
    return pl.pallas_call(
        functools.partial(_attn_kernel, tk=tk, hd=hd),
        grid=(bn, d // LANES, s // tq),
        in_specs=[pl.BlockSpec((1, tq, LANES), lambda b, h, i: (b, i, h)),
                  pl.BlockSpec((1, s, LANES), lambda b, h, i: (b, 0, h)),
                  pl.BlockSpec((1, s, LANES), lambda b, h, i: (b, 0, h)),
                  pl.BlockSpec((1, n_sub, 1, tq), lambda b, h, i: (b, h, 0, i)),
                  pl.BlockSpec((1, n_sub, s, 1), lambda b, h, i: (b, h, 0, 0))],
        out_specs=pl.BlockSpec((1, tq, LANES), lambda b, h, i: (b, i, h)),
        out_shape=jax.ShapeDtypeStruct((bn, s, d), BF16),
        scratch_shapes=[pltpu.VMEM((s // tk, n_sub, hd + 2 * SUBLANES, tk), BF16)],
        compiler_params=_cparams(("arbitrary", "arbitrary", "arbitrary")),
        name="fox_attention",
    )(q, k, v, fq, fk)


def _attn_out_kernel(o_ref, x_ref, mod_ref, w_ref, b_ref, g_ref, bb_ref, out_ref, *, alpha):
    d = x_ref.shape[2]
    y = jnp.dot(o_ref[0], w_ref[...], preferred_element_type=F32) + b_ref[...]
    _, _, gate = _split_mod(mod_ref[0, 0], d)
    out_ref[0] = _layernorm(alpha * x_ref[0] + gate * y, g_ref[...], bb_ref[...])


def _attn_out(o, x, mod, layer, w_out, b_out, g, b, alpha):
    bn, s, d = x.shape
    t = min(s, 512)
    row = lambda v: v.reshape(1, -1)
    const = lambda shape: pl.BlockSpec(shape, lambda i, j: (0,) * len(shape))
    return pl.pallas_call(
        functools.partial(_attn_out_kernel, alpha=alpha),
        grid=(bn, s // t),
        in_specs=[pl.BlockSpec((1, t, d), lambda i, j: (i, j, 0)),
                  pl.BlockSpec((1, t, d), lambda i, j: (i, j, 0)),
                  pl.BlockSpec((1, 1, 1, 3 * d), lambda i, j: (layer, i, 0, 0)),
                  const((d, d)), const((1, d)), const((1, d)), const((1, d))],
        out_specs=pl.BlockSpec((1, t, d), lambda i, j: (i, j, 0)),
        out_shape=jax.ShapeDtypeStruct((bn, s, d), F32),
        compiler_params=_cparams(("arbitrary", "arbitrary")),
        name="fox_out",
    )(o, x, mod, w_out.astype(BF16), row(b_out), row(g), row(b))


def _sort_desc(a):
    a = list(a)
    n = len(a)
    k = 2
    while k <= n:
        j = k // 2
        while j >= 1:
            for i in range(n):
                l = i ^ j
                if l > i:
                    hi, lo = jnp.maximum(a[i], a[l]), jnp.minimum(a[i], a[l])
                    a[i], a[l] = (hi, lo) if (i & k) == 0 else (lo, hi)
            j //= 2
        k *= 2
    return a


def _bitonic_merge_desc(t):
    t = list(t)
    n = len(t)
    j = n // 2
    while j >= 1:
        for i in range(n):
            l = i ^ j
            if l > i:
                t[i], t[l] = jnp.maximum(t[i], t[l]), jnp.minimum(t[i], t[l])
        j //= 2
    return t


def _merge_top(a, b):
    n = len(a)
    t = [a[i] if b[n - 1 - i] is None else jnp.maximum(a[i], b[n - 1 - i]) for i in range(n)]
    return _bitonic_merge_desc(t)


def _select_by_bits(table, bits):
    for b in bits:
        table = [jnp.where(b, table[2 * i + 1], table[2 * i]) for i in range(len(table) // 2)]
    return table[0]


def _rank_bits(v, s):
    n = len(v)
    known = []
    step = n // 2
    while step >= 1:
        pivots = [v[lo + step - 1] for lo in range(0, n, 2 * step)]
        known.append(_select_by_bits(pivots, known[::-1]) > s)
        step //= 2
    return known[::-1], v[n - 1] > s


def _top_values(s_ref, tk):
    n_keys = s_ref.shape[0]
    groups = [s_ref[g * SUBLANES:(g + 1) * SUBLANES, :] for g in range(n_keys // SUBLANES)]
    top = _sort_desc(groups[:tk])
    for g0 in range(tk, len(groups), tk):
        top = _merge_top(top, _sort_desc(groups[g0:g0 + tk]))
    shift = SUBLANES // 2
    while shift >= 1:
        top = _merge_top(top, [pltpu.roll(v, shift, 0) for v in top])
        shift //= 2
    return top


def _candidate_counts(v1, v2, tk):
    cand = {}
    for i in range(tk):
        for j in range(tk // (i + 1)):
            cand[i, j] = v1[i] + v2[j]
    half = tk // 2
    top = [cand[0, j] for j in range(tk)]
    lst_a = [cand[1, j] for j in range(half)] + [cand[i, 0] for i in range(tk - 1, half - 1, -1)]
    top = _merge_top(top, _bitonic_merge_desc(lst_a))
    rest = [cand[i, j] for i in range(2, half) for j in range(tk // (i + 1))]
    for g0 in range(0, len(rest), tk):
        grp = rest[g0:g0 + tk]
        if len(grp) == tk:
            grp = _sort_desc(grp)
        else:
            grp = _sort_desc(grp + [jnp.full_like(grp[0], NEG_INF)] * (tk - len(grp)))
        top = _merge_top(top, grp)
    tau = top[tk - 1]
    e1 = [jnp.exp(v1[i] - v1[0]) for i in range(tk)]
    e2 = [jnp.exp(v2[j] - v2[0]) for j in range(tk)]
    cnt, z = [], None
    for i in range(tk):
        c_i, z_i = None, None
        for j in range(tk // (i + 1)):
            sel = cand[i, j] >= tau
            one = jnp.where(sel, 1.0, 0.0)
            w = jnp.where(sel, e2[j], 0.0)
            c_i = one if c_i is None else c_i + one
            z_i = w if z_i is None else z_i + w
        cnt.append(c_i)
        z = e1[i] * z_i if z is None else z + e1[i] * z_i
    return cnt, 1.0 / z


def _route_kernel(x_ref, mod_ref, wq_ref, k1_ref, k2_ref,
                  ht_ref, c_ref, e1_ref, r2_ref, e2_ref,
                  q_scr, s_scr, v_scr, vh_scr, cs_scr, *, n_head, tk):
    t, d = x_ref.shape
    n_keys = k1_ref.shape[0]
    shift, scale, _ = _split_mod(mod_ref[0, 0], d)
    h = x_ref[...] * (1.0 + scale) + shift
    ht = h.T.astype(BF16)
    ht_ref[...] = ht
    q_scr[...] = jnp.dot(wq_ref[...], ht, preferred_element_type=F32).astype(BF16)

    def scores_and_top(hh, carry):
        for half, k_ref in enumerate((k1_ref, k2_ref)):
            off = pl.multiple_of((2 * hh + half) * n_keys, n_keys)
            s_scr[hh, half] = jnp.dot(k_ref[...], q_scr[pl.ds(off, n_keys), :],
                                      preferred_element_type=F32)
            top = _top_values(s_scr.at[hh, half], tk)
            for i in range(tk):
                v_scr[hh, half, i] = top[i]
                vh_scr[half, i, pl.ds(hh, 1), :] = top[i][0:1, :]
        return carry

    lax.fori_loop(0, n_head, scores_and_top, 0)

    v1 = [vh_scr[0, i] for i in range(tk)]
    v2 = [vh_scr[1, i] for i in range(tk)]
    cnt, inv_z = _candidate_counts(v1, v2, tk)
    for i in range(tk):
        cs_scr[i] = cnt[i]
    cs_scr[tk] = inv_z

    def dense(hh, carry):
        def row(idx):
            return jnp.broadcast_to(cs_scr[idx, pl.ds(hh, 1), :], (SUBLANES, t))
        cnt_h = [row(i) for i in range(tk)]
        inv = row(tk)
        v1h = [v_scr[hh, 0, i] for i in range(tk)]
        v2h = [v_scr[hh, 1, i] for i in range(tk)]
        for g in range(n_keys // SUBLANES):
            rows = slice(g * SUBLANES, (g + 1) * SUBLANES)
            s1 = s_scr[hh, 0, rows, :]
            s2 = s_scr[hh, 1, rows, :]
            bits1, below1 = _rank_bits(v1h, s1)
            bits2, below2 = _rank_bits(v2h, s2)
            c = jnp.where(below1, 0.0, _select_by_bits(cnt_h, bits1))
            r2 = jnp.where(below2, 1.0, 0.0)
            for b, weight in zip(bits2, (1.0, 2.0, 4.0, 8.0)):
                r2 = r2 + jnp.where(b, weight, 0.0)
            c_ref[hh, rows, :] = c
            e1_ref[hh, rows, :] = jnp.exp(s1 - v1h[0])
            r2_ref[hh, rows, :] = r2.astype(BF16)
            e2_ref[hh, rows, :] = (jnp.exp(s2 - v2h[0]) * inv).astype(BF16)
        return carry

    lax.fori_loop(0, n_head, dense, 0)


def _peer_route(x2, mod, layer, seq, wq_t, k1, k2, n_head):
    n_tok, d = x2.shape
    n_keys = k1.shape[0]
    t = min(n_tok, 512)
    tk = PEER_TOPK
    const = lambda shape: pl.BlockSpec(shape, lambda i: (0,) * len(shape))
    tok3 = pl.BlockSpec((n_head, n_keys, t), lambda i: (0, 0, i))
    return pl.pallas_call(
        functools.partial(_route_kernel, n_head=n_head, tk=tk),
        grid=(n_tok // t,),
        in_specs=[pl.BlockSpec((t, d), lambda i: (i, 0)),
                  pl.BlockSpec((1, 1, 1, 3 * d), lambda i: (layer, (i * t) // seq, 0, 0)),
                  const(wq_t.shape), const(k1.shape), const(k2.shape)],
        out_specs=[pl.BlockSpec((d, t), lambda i: (0, i)), tok3, tok3, tok3, tok3],
        out_shape=[jax.ShapeDtypeStruct((d, n_tok), BF16),
                   jax.ShapeDtypeStruct((n_head, n_keys, n_tok), F32),
                   jax.ShapeDtypeStruct((n_head, n_keys, n_tok), F32),
                   jax.ShapeDtypeStruct((n_head, n_keys, n_tok), BF16),
                   jax.ShapeDtypeStruct((n_head, n_keys, n_tok), BF16)],
        scratch_shapes=[pltpu.VMEM((2 * n_head * n_keys, t), BF16),
                        pltpu.VMEM((n_head, 2, n_keys, t), F32),
                        pltpu.VMEM((n_head, 2, tk, SUBLANES, t), F32),
                        pltpu.VMEM((2, tk, SUBLANES, t), F32),
                        pltpu.VMEM((tk + 1, SUBLANES, t), F32)],
        compiler_params=_cparams(("arbitrary",)),
        name="peer_route",
    )(x2, mod, wq_t, k1, k2)


def _peer_dense_kernel(ht_ref, u_ref, un_ref, vt_ref, vp_ref, c_ref, e1_ref, r2_ref, e2_ref,
                       x_ref, mod_ref, g_ref, b_ref, o_ref, a0, a1, w0, w1, acc_scr,
                       *, n_head, n_keys, alpha):
    t, d = x_ref.shape
    chunk = a0.shape[0]
    n_c = u_ref.shape[0] // chunk
    n_a = chunk // n_keys
    assert n_c % 2 == 0
    pack = 2 * SUBLANES
    k = pl.program_id(1)
    a_buf, w_buf = (a0, a1), (w0, w1)

    def u_matmul(u_rows, dst):
        dst[...] = jnp.dot(u_rows, ht_ref[...], preferred_element_type=F32)

    def v_matmul(vt_chunk, w_scr):
        acc_scr[...] += jnp.dot(vt_chunk, w_scr[...], preferred_element_type=F32)

    @pl.when(k == 0)
    def _():
        w1[...] = jnp.zeros(w1.shape, w1.dtype)
        acc_scr[...] = jnp.zeros(acc_scr.shape, F32)
        u_matmul(u_ref[0:chunk, :], a0)

    def gate_pass(a_scr, w_scr, c):
        for a in range(n_a):
            key = (k * n_c + c) * n_a + a
            pre = a_scr[a * n_keys:(a + 1) * n_keys, :]
            act = (0.5 * pre * (1.0 + lax.erf(pre * INV_SQRT2))).astype(BF16)
            gate = [jnp.zeros((pack, t), BF16) for _ in range(n_keys // pack)]
            for hh in range(n_head):
                cnt = jnp.broadcast_to(c_ref[hh, pl.ds(key, 1), :], (pack, t)).astype(BF16)
                e1 = jnp.broadcast_to(e1_ref[hh, pl.ds(key, 1), :], (pack, t)).astype(BF16)
                for bc in range(n_keys // pack):
                    rows = slice(bc * pack, (bc + 1) * pack)
                    sel = jnp.where(r2_ref[hh, rows, :] < cnt, e2_ref[hh, rows, :],
                                    jnp.zeros((), BF16))
                    gate[bc] = gate[bc] + sel * e1
            for bc in range(n_keys // pack):
                r0 = a * n_keys + bc * pack
                w_scr[r0:r0 + pack, :] = act[bc * pack:(bc + 1) * pack, :] * gate[bc]

    for c in range(n_c):
        u_next = un_ref[...] if c == n_c - 1 else u_ref[(c + 1) * chunk:(c + 2) * chunk, :]
        u_matmul(u_next, a_buf[(c + 1) % 2])
        gate_pass(a_buf[c % 2], w_buf[c % 2], c)
        v_matmul(vp_ref[0] if c == 0 else vt_ref[c - 1], w_buf[(c - 1) % 2])

    @pl.when(k == pl.num_programs(1) - 1)
    def _():
        v_matmul(vt_ref[n_c - 1], w_buf[(n_c - 1) % 2])
        _, _, gate_mod = _split_mod(mod_ref[0, 0], d)
        y = acc_scr[...].T
        o_ref[...] = _layernorm(alpha * x_ref[...] + gate_mod * y, g_ref[...], b_ref[...])


def _peer_dense(x2, mod, layer, seq, ht, c, e1, r2, e2, u_bf, v_bf, g, b, alpha):
    n_tok, d = x2.shape
    n_head, n_keys, _ = c.shape
    n_exp = u_bf.shape[0]
    t = min(n_tok, 512)
    chunk = 2 * n_keys
    n_c = 8
    n_chunk = n_exp // chunk
    n_k = n_chunk // n_c
    vt3 = v_bf.reshape(n_chunk, chunk, d).transpose(0, 2, 1)
    tok3 = pl.BlockSpec((n_head, n_keys, t), lambda i, j: (0, 0, i))
    row = lambda v: v.reshape(1, -1)
    return pl.pallas_call(
        functools.partial(_peer_dense_kernel, n_head=n_head, n_keys=n_keys, alpha=alpha),
        grid=(n_tok // t, n_k),
        in_specs=[pl.BlockSpec((d, t), lambda i, j: (0, i)),
                  pl.BlockSpec((n_c * chunk, d), lambda i, j: (j, 0)),
                  pl.BlockSpec((chunk, d), lambda i, j: (jnp.minimum((j + 1) * n_c, n_chunk - 1), 0)),
                  pl.BlockSpec((n_c, d, chunk), lambda i, j: (j, 0, 0)),
                  pl.BlockSpec((1, d, chunk), lambda i, j: (jnp.maximum(j * n_c - 1, 0), 0, 0)),
                  tok3, tok3, tok3, tok3,
                  pl.BlockSpec((t, d), lambda i, j: (i, 0)),
                  pl.BlockSpec((1, 1, 1, 3 * d), lambda i, j: (layer, (i * t) // seq, 0, 0)),
                  pl.BlockSpec((1, d), lambda i, j: (0, 0)),
                  pl.BlockSpec((1, d), lambda i, j: (0, 0))],
        out_specs=pl.BlockSpec((t, d), lambda i, j: (i, 0)),
        out_shape=jax.ShapeDtypeStruct((n_tok, d), F32),
        scratch_shapes=[pltpu.VMEM((chunk, t), F32), pltpu.VMEM((chunk, t), F32),
                        pltpu.VMEM((chunk, t), BF16), pltpu.VMEM((chunk, t), BF16),
                        pltpu.VMEM((d, t), F32)],
        compiler_params=_cparams(("arbitrary", "arbitrary")),
        name="peer_dense",
    )(ht, u_bf, u_bf, vt3, vt3, c, e1, r2, e2, x2, mod, row(g), row(b))


def _peer_layer(x, mod, layer, wq, k1, k2, u, v, g, b, alpha):
    bn, s, d = x.shape
    n_keys, half_key = k1.shape
    n_head = wq.shape[1] // (2 * half_key)
    x2 = x.reshape(bn * s, d)
    ht, c, e1, r2, e2 = _peer_route(x2, mod, layer, s, wq.T.astype(BF16), k1.astype(BF16),
                                    k2.astype(BF16), n_head)
    out = _peer_dense(x2, mod, layer, s, ht, c, e1, r2, e2, u.astype(BF16), v.astype(BF16),
                      g, b, alpha)
    return out.reshape(bn, s, d)


def kernel(x, c, ada_mix_w, ada_mix_b, ln_mix_g, ln_mix_b, conv_in_w, conv_in_b, conv_dw_w, conv_dw_b, conv_ln_g, conv_ln_b, conv_out_w, conv_out_b, attn_in_w, attn_in_b, attn_out_w, attn_out_b, ada_ffn_w, ada_ffn_b, ln_ffn_g, ln_ffn_b, peer_query_w, peer_sub_keys_1, peer_sub_keys_2, peer_expert_u, peer_expert_v):
    depth = ada_mix_w.shape[0]
    d = x.shape[-1]
    n_attn_head = attn_in_w.shape[-1] - 3 * d
    alpha = (2 * depth) ** 0.25
    mod_mix = _adaln(c, ada_mix_w, ada_mix_b)
    mod_ffn = _adaln(c, ada_ffn_w, ada_ffn_b)
    for i in range(depth):
        j = i // 2
        if i % 2 == 0:
            x = _conv_mixer(x, mod_mix, i, conv_in_w[j], conv_in_b[j], conv_dw_w[j], conv_dw_b[j],
                            conv_ln_g[j], conv_ln_b[j], conv_out_w[j], conv_out_b[j],
                            ln_mix_g[i], ln_mix_b[i], alpha)
        else:
            q, k, v, cum = _qkv(x, mod_mix, i, attn_in_w[j], attn_in_b[j], n_attn_head)
            cum_t = jnp.swapaxes(cum[:, :, :n_attn_head], 1, 2)
            o = _attention(q, k, v, cum_t[:, :, None, :], cum_t[..., None], d // n_attn_head)
            x = _attn_out(o, x, mod_mix, i, attn_out_w[j], attn_out_b[j],
                          ln_mix_g[i], ln_mix_b[i], alpha)
        x = _peer_layer(x, mod_ffn, i, peer_query_w[i], peer_sub_keys_1[i], peer_sub_keys_2[i],
                        peer_expert_u[i], peer_expert_v[i], ln_ffn_g[i], ln_ffn_b[i], alpha)
    return x
```

```python
import functools
import math

import jax
import jax.numpy as jnp
from jax import lax
from jax.experimental import pallas as pl
from jax.experimental.pallas import tpu as pltpu

F32 = jnp.float32
BF16 = jnp.bfloat16

LN_EPS = 1e-5
CONV_HALO = 32
PEER_TOPK = 16
TOKEN_TILE = 512
ATTN_Q_TILE = 1024
ATTN_K_TILE = 512
ADALN_COL_TILE = 1024
PEER_STAGE_KEYS = 2
PEER_STAGES_PER_STEP = 8
V7X_VMEM_LIMIT_BYTES = 56 * 1024 * 1024
SUBLANES = 8
LANES = 128
INV_SQRT2 = 1.0 / math.sqrt(2.0)
NEG_INF = float("-inf")


def _cparams(sem):
    return pltpu.CompilerParams(dimension_semantics=sem, vmem_limit_bytes=V7X_VMEM_LIMIT_BYTES)


def _layernorm(z, g, b):
    mu = jnp.mean(z, axis=-1, keepdims=True)
    zc = z - mu
    var = jnp.mean(zc * zc, axis=-1, keepdims=True)
    return zc * lax.rsqrt(var + LN_EPS) * g + b


def _split_mod(mod, d):
    return mod[:, :d], mod[:, d:2 * d], mod[:, 2 * d:]


def _ada_kernel(c_ref, w_ref, b_ref, o_ref):
    c = c_ref[...]
    sc = (c * jax.nn.sigmoid(c)).astype(BF16)
    o_ref[0] = jnp.dot(sc, w_ref[0].astype(BF16), preferred_element_type=F32) + b_ref[0]


def _adaln(c, w, b):
    n_layer, d, d3 = w.shape
    bn = c.shape[0]
    tn = min(d3, ADALN_COL_TILE)
    out = pl.pallas_call(
        _ada_kernel,
        grid=(n_layer, d3 // tn),
        in_specs=[pl.BlockSpec((bn, d), lambda l, j: (0, 0)),
                  pl.BlockSpec((1, d, tn), lambda l, j: (l, 0, j)),
                  pl.BlockSpec((1, 1, tn), lambda l, j: (l, 0, j))],
        out_specs=pl.BlockSpec((1, bn, tn), lambda l, j: (l, 0, j)),
        out_shape=jax.ShapeDtypeStruct((n_layer, bn, d3), F32),
        compiler_params=_cparams(("arbitrary", "arbitrary")),
        name="adaln",
    )(c, w, b.reshape(n_layer, 1, d3))
    return out.reshape(n_layer, bn, 1, d3)


def _conv_kernel(x_ref, mod_ref, w_in_ref, b_in_ref, w_dw_ref, b_dw_ref, cg_ref, cb_ref,
                 w_out_ref, b_out_ref, g_ref, b_ref, o_ref, abuf, shbuf, *, width, alpha):
    t, d = x_ref.shape[1], x_ref.shape[2]

    @pl.when(pl.program_id(1) == 0)
    def _():
        abuf[0:CONV_HALO, :] = jnp.zeros((CONV_HALO, d), F32)

    x = x_ref[0]
    shift, scale, gate = _split_mod(mod_ref[0, 0], d)
    h = (x * (1.0 + scale) + shift).astype(BF16)
    p = jnp.dot(h, w_in_ref[...], preferred_element_type=F32) + b_in_ref[...]
    abuf[CONV_HALO:CONV_HALO + t, :] = p[:, :d] * jax.nn.sigmoid(p[:, d:])
    base = CONV_HALO - (width - 1)
    n_sh = shbuf.shape[1]
    for s in range(1, SUBLANES):
        shbuf[s - 1] = abuf[s:s + n_sh, :]
    acc = jnp.zeros((t, d), F32) + b_dw_ref[...]
    for k in range(width):
        row0, s = (base + k) // SUBLANES * SUBLANES, (base + k) % SUBLANES
        tap = abuf[row0:row0 + t, :] if s == 0 else shbuf[s - 1, row0:row0 + t, :]
        acc = acc + w_dw_ref[k:k + 1, :] * tap
    abuf[0:CONV_HALO, :] = abuf[t:t + CONV_HALO, :]
    a = _layernorm(acc, cg_ref[...], cb_ref[...])
    a = (a * jax.nn.sigmoid(a)).astype(BF16)
    y = jnp.dot(a, w_out_ref[...], preferred_element_type=F32) + b_out_ref[...]
    o_ref[0] = _layernorm(alpha * x + gate * y, g_ref[...], b_ref[...])


def _conv_mixer(x, mod, layer, w_in, b_in, w_dw, b_dw, cg, cb, w_out, b_out, g, b, alpha):
    bn, s, d = x.shape
    t = min(s, TOKEN_TILE)
    width = w_dw.shape[0]
    row = lambda v: v.reshape(1, -1)
    const = lambda shape: pl.BlockSpec(shape, lambda i, j: (0,) * len(shape))
    return pl.pallas_call(
        functools.partial(_conv_kernel, width=width, alpha=alpha),
        grid=(bn, s // t),
        in_specs=[pl.BlockSpec((1, t, d), lambda i, j: (i, j, 0)),
                  pl.BlockSpec((1, 1, 1, 3 * d), lambda i, j: (layer, i, 0, 0)),
                  const((d, 2 * d)), const((1, 2 * d)), const((width, d)), const((1, d)),
                  const((1, d)), const((1, d)), const((d, d)), const((1, d)),
                  const((1, d)), const((1, d))],
        out_specs=pl.BlockSpec((1, t, d), lambda i, j: (i, j, 0)),
        out_shape=jax.ShapeDtypeStruct((bn, s, d), F32),
        scratch_shapes=[pltpu.VMEM((t + CONV_HALO, d), F32),
                        pltpu.VMEM((SUBLANES - 1, t + CONV_HALO - SUBLANES, d), F32)],
        compiler_params=_cparams(("arbitrary", "arbitrary")),
        name="conv_mixer",
    )(x, mod, w_in.astype(BF16), row(b_in), w_dw, row(b_dw), row(cg), row(cb),
      w_out.astype(BF16), row(b_out), row(g), row(b))


def _qkv_kernel(x_ref, mod_ref, w_ref, b_ref, wf_ref, bf_ref, q_ref, k_ref, v_ref, cum_ref, carry,
                *, q_scale):
    t, d = x_ref.shape[1], x_ref.shape[2]

    @pl.when(pl.program_id(1) == 0)
    def _():
        carry[...] = jnp.zeros(carry.shape, F32)

    shift, scale, _ = _split_mod(mod_ref[0, 0], d)
    h = (x_ref[0] * (1.0 + scale) + shift).astype(BF16)
    proj = jnp.dot(h, w_ref[...], preferred_element_type=F32) + b_ref[...]
    q_ref[0] = (proj[:, :d] * q_scale).astype(BF16)
    k_ref[0] = proj[:, d:2 * d].astype(BF16)
    v_ref[0] = proj[:, 2 * d:].astype(BF16)
    f = jnp.dot(h, wf_ref[...], preferred_element_type=F32) + bf_ref[...]
    log_f = jax.nn.log_sigmoid(f)
    r = lax.broadcasted_iota(jnp.int32, (t, t), 0)
    c = lax.broadcasted_iota(jnp.int32, (t, t), 1)
    tri = jnp.where(c <= r, 1.0, 0.0).astype(BF16)
    hi = log_f.astype(BF16)
    rem = log_f - hi.astype(F32)
    mid = rem.astype(BF16)
    lo = (rem - mid.astype(F32)).astype(BF16)
    cum = (jnp.dot(tri, hi, preferred_element_type=F32)
           + jnp.dot(tri, mid, preferred_element_type=F32)
           + jnp.dot(tri, lo, preferred_element_type=F32)) + carry[0:1, :]
    cum_ref[0] = cum
    carry[0:1, :] = cum[t - 1:t, :]


def _qkv(x, mod, layer, w_in, b_in, n_head):
    bn, s, d = x.shape
    hd = d // n_head
    t = min(s, TOKEN_TILE)
    w_qkv = w_in[:, :3 * d].astype(BF16)
    b_qkv = b_in[:3 * d].reshape(1, 3 * d)
    w_f = jnp.pad(w_in[:, 3 * d:], ((0, 0), (0, LANES - n_head))).astype(BF16)
    b_f = jnp.pad(b_in[3 * d:], (0, LANES - n_head)).reshape(1, LANES)
    const = lambda shape: pl.BlockSpec(shape, lambda i, j: (0,) * len(shape))
    head_spec = pl.BlockSpec((1, t, d), lambda i, j: (i, j, 0))
    head_shape = jax.ShapeDtypeStruct((bn, s, d), BF16)
    return pl.pallas_call(
        functools.partial(_qkv_kernel, q_scale=hd ** -0.5),
        grid=(bn, s // t),
        in_specs=[pl.BlockSpec((1, t, d), lambda i, j: (i, j, 0)),
                  pl.BlockSpec((1, 1, 1, 3 * d), lambda i, j: (layer, i, 0, 0)),
                  const((d, 3 * d)), const((1, 3 * d)), const((d, LANES)), const((1, LANES))],
        out_specs=[head_spec, head_spec, head_spec,
                   pl.BlockSpec((1, t, LANES), lambda i, j: (i, j, 0))],
        out_shape=[head_shape, head_shape, head_shape,
                   jax.ShapeDtypeStruct((bn, s, LANES), F32)],
        scratch_shapes=[pltpu.VMEM((SUBLANES, LANES), F32)],
        compiler_params=_cparams(("arbitrary", "arbitrary")),
        name="fox_qkv",
    )(x, mod, w_qkv, b_qkv, w_f, b_f)


def _attn_kernel(q_ref, k_ref, v_ref, fq_ref, fk_ref, o_ref, vt_aug, *, tk, hd):
    tq, bw = q_ref.shape[1], q_ref.shape[2]
    n_sub = bw // hd
    n_aug = vt_aug.shape[2]
    qi = pl.program_id(2)

    @pl.when(qi == 0)
    def _():
        row = lax.broadcasted_iota(jnp.int32, (n_aug - hd, tk), 0)
        ones_row = jnp.where(row == 0, 1.0, 0.0).astype(vt_aug.dtype)
        for kv in range(vt_aug.shape[0]):
            v_t = v_ref[0, kv * tk:(kv + 1) * tk, :].astype(F32).T.astype(vt_aug.dtype)
            for j in range(n_sub):
                vt_aug[kv, j, 0:hd, :] = v_t[j * hd:(j + 1) * hd, :]
                vt_aug[kv, j, hd:n_aug, :] = ones_row

    q2 = q_ref[0]
    lane = lax.broadcasted_iota(jnp.int32, (tq, bw), 1)
    qh = [jnp.where((lane >= j * hd) & (lane < (j + 1) * hd), q2, jnp.zeros((), q2.dtype))
          for j in range(n_sub)]
    fq = [fq_ref[0, j] for j in range(n_sub)]

    def step(kv, carry, masked, q_lo=0):
        off = pl.multiple_of(kv * tk, tk)
        k = k_ref[0, pl.ds(off, tk), :]
        nq = tq - q_lo
        out = []
        for j in range(n_sub):
            m_all, acc_all = carry[j]
            m, acc, fq_j = m_all[:, q_lo:], acc_all[:, q_lo:], fq[j][:, q_lo:]
            z = lax.dot_general(k, qh[j][q_lo:, :], (((1,), (1,)), ((), ())),
                                preferred_element_type=F32)
            z = z - fk_ref[0, j, pl.ds(off, tk), :]
            if masked:
                key = off + lax.broadcasted_iota(jnp.int32, (tk, tk), 0)
                qry = qi * tq + q_lo + lax.broadcasted_iota(jnp.int32, (tk, tk), 1)
                z_diag = jnp.where(key <= qry, z[:, :tk], NEG_INF)
                z = z_diag if nq == tk else jnp.concatenate([z_diag, z[:, tk:]], axis=1)
            m_new = jnp.maximum(m, jnp.max(z, axis=0, keepdims=True) + fq_j)
            a = jnp.exp(m - m_new)
            p = jnp.exp((z - (m_new - fq_j)).astype(BF16))
            acc = a * acc + jnp.dot(vt_aug[kv, j], p, preferred_element_type=F32)
            if q_lo:
                m_new = jnp.concatenate([m_all[:, :q_lo], m_new], axis=1)
                acc = jnp.concatenate([acc_all[:, :q_lo], acc], axis=1)
            out.append((m_new, acc))
        return tuple(out)

    carry = tuple((jnp.full((1, tq), NEG_INF, F32), jnp.zeros((n_aug, tq), F32))
                  for _ in range(n_sub))
    n_diag = tq // tk
    carry = lax.fori_loop(0, qi * n_diag, lambda kv, c: step(kv, c, False), carry)
    for i in range(n_diag):
        carry = step(qi * n_diag + i, carry, True, q_lo=i * tk)
    res = carry
    o_t = jnp.concatenate([acc[:hd, :] / acc[hd:hd + 1, :] for _, acc in res], axis=0)
    o_ref[0] = o_t.T.astype(o_ref.dtype)


def _attention(q, k, v, fq, fk, hd):
    bn, s, d = q.shape
    bw = LANES
    n_sub = bw // hd
    tk = min(s, ATTN_K_TILE)
    tq = min(s, ATTN_Q_TILE)
    return pl.pallas_call(
        functools.partial(_attn_kernel, tk=tk, hd=hd),
        grid=(bn, d // bw, s // tq),
        in_specs=[pl.BlockSpec((1, tq, bw), lambda b, h, i: (b, i, h)),
                  pl.BlockSpec((1, s, bw), lambda b, h, i: (b, 0, h)),
                  pl.BlockSpec((1, s, bw), lambda b, h, i: (b, 0, h)),
                  pl.BlockSpec((1, n_sub, 1, tq), lambda b, h, i: (b, h, 0, i)),
                  pl.BlockSpec((1, n_sub, s, 1), lambda b, h, i: (b, h, 0, 0))],
        out_specs=pl.BlockSpec((1, tq, bw), lambda b, h, i: (b, i, h)),
        out_shape=jax.ShapeDtypeStruct((bn, s, d), BF16),
        scratch_shapes=[pltpu.VMEM((s // tk, n_sub, hd + 2 * SUBLANES, tk), BF16)],
        compiler_params=_cparams(("arbitrary", "arbitrary", "arbitrary")),
        name="fox_attention",
    )(q, k, v, fq, fk)


def _attn_out_kernel(o_ref, x_ref, mod_ref, w_ref, b_ref, g_ref, bb_ref, out_ref, *, alpha):
    d = x_ref.shape[2]
    y = jnp.dot(o_ref[0], w_ref[...], preferred_element_type=F32) + b_ref[...]
    _, _, gate = _split_mod(mod_ref[0, 0], d)
    out_ref[0] = _layernorm(alpha * x_ref[0] + gate * y, g_ref[...], bb_ref[...])


def _attn_out(o, x, mod, layer, w_out, b_out, g, b, alpha):
    bn, s, d = x.shape
    t = min(s, TOKEN_TILE)
    row = lambda v: v.reshape(1, -1)
    const = lambda shape: pl.BlockSpec(shape, lambda i, j: (0,) * len(shape))
    return pl.pallas_call(
        functools.partial(_attn_out_kernel, alpha=alpha),
        grid=(bn, s // t),
        in_specs=[pl.BlockSpec((1, t, d), lambda i, j: (i, j, 0)),
                  pl.BlockSpec((1, t, d), lambda i, j: (i, j, 0)),
                  pl.BlockSpec((1, 1, 1, 3 * d), lambda i, j: (layer, i, 0, 0)),
                  const((d, d)), const((1, d)), const((1, d)), const((1, d))],
        out_specs=pl.BlockSpec((1, t, d), lambda i, j: (i, j, 0)),
        out_shape=jax.ShapeDtypeStruct((bn, s, d), F32),
        compiler_params=_cparams(("arbitrary", "arbitrary")),
        name="fox_out",
    )(o, x, mod, w_out.astype(BF16), row(b_out), row(g), row(b))


def _sort_desc(a):
    a = list(a)
    n = len(a)
    k = 2
    while k <= n:
        j = k // 2
        while j >= 1:
            for i in range(n):
                l = i ^ j
                if l > i:
                    hi, lo = jnp.maximum(a[i], a[l]), jnp.minimum(a[i], a[l])
                    a[i], a[l] = (hi, lo) if (i & k) == 0 else (lo, hi)
            j //= 2
        k *= 2
    return a


def _bitonic_merge_desc(t):
    t = list(t)
    n = len(t)
    j = n // 2
    while j >= 1:
        for i in range(n):
            l = i ^ j
            if l > i:
                t[i], t[l] = jnp.maximum(t[i], t[l]), jnp.minimum(t[i], t[l])
        j //= 2
    return t


def _merge_top(a, b):
    n = len(a)
    t = [a[i] if b[n - 1 - i] is None else jnp.maximum(a[i], b[n - 1 - i]) for i in range(n)]
    return _bitonic_merge_desc(t)


def _select_by_bits(table, bits):
    for b in bits:
        table = [jnp.where(b, table[2 * i + 1], table[2 * i]) for i in range(len(table) // 2)]
    return table[0]


def _rank_bits(v, s):
    n = len(v)
    known = []
    step = n // 2
    while step >= 1:
        pivots = [v[lo + step - 1] for lo in range(0, n, 2 * step)]
        known.append(_select_by_bits(pivots, known[::-1]) > s)
        step //= 2
    return known[::-1], v[n - 1] > s


def _top_values(s_ref, tk):
    n_keys = s_ref.shape[0]
    groups = [s_ref[g * SUBLANES:(g + 1) * SUBLANES, :] for g in range(n_keys // SUBLANES)]
    top = _sort_desc(groups[:tk])
    for g0 in range(tk, len(groups), tk):
        top = _merge_top(top, _sort_desc(groups[g0:g0 + tk]))
    shift = SUBLANES // 2
    while shift >= 1:
        top = _merge_top(top, [pltpu.roll(v, shift, 0) for v in top])
        shift //= 2
    return top


def _candidate_counts(v1, v2, tk):
    cand = {}
    for i in range(tk):
        for j in range(tk // (i + 1)):
            cand[i, j] = v1[i] + v2[j]
    half = tk // 2
    top = [cand[0, j] for j in range(tk)]
    lst_a = [cand[1, j] for j in range(half)] + [cand[i, 0] for i in range(tk - 1, half - 1, -1)]
    top = _merge_top(top, _bitonic_merge_desc(lst_a))
    rest = [cand[i, j] for i in range(2, half) for j in range(tk // (i + 1))]
    for g0 in range(0, len(rest), tk):
        grp = rest[g0:g0 + tk]
        if len(grp) == tk:
            grp = _sort_desc(grp)
        else:
            grp = _sort_desc(grp + [jnp.full_like(grp[0], NEG_INF)] * (tk - len(grp)))
        top = _merge_top(top, grp)
    tau = top[tk - 1]
    e1 = [jnp.exp(v1[i] - v1[0]) for i in range(tk)]
    e2 = [jnp.exp(v2[j] - v2[0]) for j in range(tk)]
    cnt, z = [], None
    for i in range(tk):
        c_i, z_i = None, None
        for j in range(tk // (i + 1)):
            sel = cand[i, j] >= tau
            one = jnp.where(sel, 1.0, 0.0)
            w = jnp.where(sel, e2[j], 0.0)
            c_i = one if c_i is None else c_i + one
            z_i = w if z_i is None else z_i + w
        cnt.append(c_i)
        z = e1[i] * z_i if z is None else z + e1[i] * z_i
    return cnt, 1.0 / z


def _route_kernel(x_ref, mod_ref, wq_ref, k1_ref, k2_ref,
                  ht_ref, c_ref, e1_ref, r2_ref, e2_ref,
                  q_scr, s_scr, v_scr, vh_scr, cs_scr, *, n_head, tk):
    t, d = x_ref.shape
    n_keys = k1_ref.shape[0]
    shift, scale, _ = _split_mod(mod_ref[0, 0], d)
    h = x_ref[...] * (1.0 + scale) + shift
    ht = h.T.astype(BF16)
    ht_ref[...] = ht
    q_scr[...] = jnp.dot(wq_ref[...], ht, preferred_element_type=F32).astype(BF16)

    def scores_and_top(hh, carry):
        for half, k_ref in enumerate((k1_ref, k2_ref)):
            off = pl.multiple_of((2 * hh + half) * n_keys, n_keys)
            s_scr[hh, half] = jnp.dot(k_ref[...], q_scr[pl.ds(off, n_keys), :],
                                      preferred_element_type=F32)
            top = _top_values(s_scr.at[hh, half], tk)
            for i in range(tk):
                v_scr[hh, half, i] = top[i]
                vh_scr[half, i, pl.ds(hh, 1), :] = top[i][0:1, :]
        return carry

    lax.fori_loop(0, n_head, scores_and_top, 0)

    v1 = [vh_scr[0, i] for i in range(tk)]
    v2 = [vh_scr[1, i] for i in range(tk)]
    cnt, inv_z = _candidate_counts(v1, v2, tk)
    for i in range(tk):
        cs_scr[i] = cnt[i]
    cs_scr[tk] = inv_z

    def dense(hh, carry):
        def row(idx):
            return jnp.broadcast_to(cs_scr[idx, pl.ds(hh, 1), :], (SUBLANES, t))
        cnt_h = [row(i) for i in range(tk)]
        inv = row(tk)
        v1h = [v_scr[hh, 0, i] for i in range(tk)]
        v2h = [v_scr[hh, 1, i] for i in range(tk)]
        for g in range(n_keys // SUBLANES):
            rows = slice(g * SUBLANES, (g + 1) * SUBLANES)
            s1 = s_scr[hh, 0, rows, :]
            s2 = s_scr[hh, 1, rows, :]
            bits1, below1 = _rank_bits(v1h, s1)
            bits2, below2 = _rank_bits(v2h, s2)
            c = jnp.where(below1, 0.0, _select_by_bits(cnt_h, bits1))
            r2 = jnp.where(below2, 1.0, 0.0)
            for b, weight in zip(bits2, (1.0, 2.0, 4.0, 8.0)):
                r2 = r2 + jnp.where(b, weight, 0.0)
            c_ref[hh, rows, :] = c
            e1_ref[hh, rows, :] = jnp.exp(s1 - v1h[0])
            r2_ref[hh, rows, :] = r2.astype(BF16)
            e2_ref[hh, rows, :] = (jnp.exp(s2 - v2h[0]) * inv).astype(BF16)
        return carry

    lax.fori_loop(0, n_head, dense, 0)


def _peer_route(x2, mod, layer, seq, wq_t, k1, k2, n_head):
    n_tok, d = x2.shape
    n_keys = k1.shape[0]
    t = min(n_tok, TOKEN_TILE)
    tk = PEER_TOPK
    const = lambda shape: pl.BlockSpec(shape, lambda i: (0,) * len(shape))
    tok3 = pl.BlockSpec((n_head, n_keys, t), lambda i: (0, 0, i))
    return pl.pallas_call(
        functools.partial(_route_kernel, n_head=n_head, tk=tk),
        grid=(n_tok // t,),
        in_specs=[pl.BlockSpec((t, d), lambda i: (i, 0)),
                  pl.BlockSpec((1, 1, 1, 3 * d), lambda i: (layer, (i * t) // seq, 0, 0)),
                  const(wq_t.shape), const(k1.shape), const(k2.shape)],
        out_specs=[pl.BlockSpec((d, t), lambda i: (0, i)), tok3, tok3, tok3, tok3],
        out_shape=[jax.ShapeDtypeStruct((d, n_tok), BF16),
                   jax.ShapeDtypeStruct((n_head, n_keys, n_tok), F32),
                   jax.ShapeDtypeStruct((n_head, n_keys, n_tok), F32),
                   jax.ShapeDtypeStruct((n_head, n_keys, n_tok), BF16),
                   jax.ShapeDtypeStruct((n_head, n_keys, n_tok), BF16)],
        scratch_shapes=[pltpu.VMEM((2 * n_head * n_keys, t), BF16),
                        pltpu.VMEM((n_head, 2, n_keys, t), F32),
                        pltpu.VMEM((n_head, 2, tk, SUBLANES, t), F32),
                        pltpu.VMEM((2, tk, SUBLANES, t), F32),
                        pltpu.VMEM((tk + 1, SUBLANES, t), F32)],
        compiler_params=_cparams(("arbitrary",)),
        name="peer_route",
    )(x2, mod, wq_t, k1, k2)


def _peer_dense_kernel(ht_ref, u_ref, un_ref, vt_ref, vp_ref, c_ref, e1_ref, r2_ref, e2_ref,
                       x_ref, mod_ref, g_ref, b_ref, o_ref, a0, a1, w0, w1, acc_scr,
                       *, n_head, n_keys, alpha):
    t, d = x_ref.shape
    chunk = a0.shape[0]
    n_c = u_ref.shape[0] // chunk
    n_a = chunk // n_keys
    assert n_c % 2 == 0
    pack = 2 * SUBLANES
    k = pl.program_id(1)
    a_buf, w_buf = (a0, a1), (w0, w1)

    def u_matmul(u_rows, dst):
        dst[...] = jnp.dot(u_rows, ht_ref[...], preferred_element_type=F32)

    def v_matmul(vt_chunk, w_scr):
        acc_scr[...] += jnp.dot(vt_chunk, w_scr[...], preferred_element_type=F32)

    @pl.when(k == 0)
    def _():
        w1[...] = jnp.zeros(w1.shape, w1.dtype)
        acc_scr[...] = jnp.zeros(acc_scr.shape, F32)
        u_matmul(u_ref[0:chunk, :], a0)

    def gate_pass(a_scr, w_scr, c):
        for a in range(n_a):
            key = (k * n_c + c) * n_a + a
            pre = a_scr[a * n_keys:(a + 1) * n_keys, :]
            act = (0.5 * pre * (1.0 + lax.erf(pre * INV_SQRT2))).astype(BF16)
            gate = [jnp.zeros((pack, t), BF16) for _ in range(n_keys // pack)]
            for hh in range(n_head):
                cnt = jnp.broadcast_to(c_ref[hh, pl.ds(key, 1), :], (pack, t)).astype(BF16)
                e1 = jnp.broadcast_to(e1_ref[hh, pl.ds(key, 1), :], (pack, t)).astype(BF16)
                for bc in range(n_keys // pack):
                    rows = slice(bc * pack, (bc + 1) * pack)
                    sel = jnp.where(r2_ref[hh, rows, :] < cnt, e2_ref[hh, rows, :],
                                    jnp.zeros((), BF16))
                    gate[bc] = gate[bc] + sel * e1
            for bc in range(n_keys // pack):
                r0 = a * n_keys + bc * pack
                w_scr[r0:r0 + pack, :] = act[bc * pack:(bc + 1) * pack, :] * gate[bc]

    for c in range(n_c):
        u_next = un_ref[...] if c == n_c - 1 else u_ref[(c + 1) * chunk:(c + 2) * chunk, :]
        u_matmul(u_next, a_buf[(c + 1) % 2])
        gate_pass(a_buf[c % 2], w_buf[c % 2], c)
        v_matmul(vp_ref[0] if c == 0 else vt_ref[c - 1], w_buf[(c - 1) % 2])

    @pl.when(k == pl.num_programs(1) - 1)
    def _():
        v_matmul(vt_ref[n_c - 1], w_buf[(n_c - 1) % 2])
        _, _, gate_mod = _split_mod(mod_ref[0, 0], d)
        y = acc_scr[...].T
        o_ref[...] = _layernorm(alpha * x_ref[...] + gate_mod * y, g_ref[...], b_ref[...])


def _peer_dense(x2, mod, layer, seq, ht, c, e1, r2, e2, u_bf, v_bf, g, b, alpha):
    n_tok, d = x2.shape
    n_head, n_keys, _ = c.shape
    n_exp = u_bf.shape[0]
    t = min(n_tok, TOKEN_TILE)
    chunk = PEER_STAGE_KEYS * n_keys
    n_c = PEER_STAGES_PER_STEP
    n_chunk = n_exp // chunk
    n_k = n_chunk // n_c
    vt3 = v_bf.reshape(n_chunk, chunk, d).transpose(0, 2, 1)
    tok3 = pl.BlockSpec((n_head, n_keys, t), lambda i, j: (0, 0, i))
    row = lambda v: v.reshape(1, -1)
    return pl.pallas_call(
        functools.partial(_peer_dense_kernel, n_head=n_head, n_keys=n_keys, alpha=alpha),
        grid=(n_tok // t, n_k),
        in_specs=[pl.BlockSpec((d, t), lambda i, j: (0, i)),
                  pl.BlockSpec((n_c * chunk, d), lambda i, j: (j, 0)),
                  pl.BlockSpec((chunk, d), lambda i, j: (jnp.minimum((j + 1) * n_c, n_chunk - 1), 0)),
                  pl.BlockSpec((n_c, d, chunk), lambda i, j: (j, 0, 0)),
                  pl.BlockSpec((1, d, chunk), lambda i, j: (jnp.maximum(j * n_c - 1, 0), 0, 0)),
                  tok3, tok3, tok3, tok3,
                  pl.BlockSpec((t, d), lambda i, j: (i, 0)),
                  pl.BlockSpec((1, 1, 1, 3 * d), lambda i, j: (layer, (i * t) // seq, 0, 0)),
                  pl.BlockSpec((1, d), lambda i, j: (0, 0)),
                  pl.BlockSpec((1, d), lambda i, j: (0, 0))],
        out_specs=pl.BlockSpec((t, d), lambda i, j: (i, 0)),
        out_shape=jax.ShapeDtypeStruct((n_tok, d), F32),
        scratch_shapes=[pltpu.VMEM((chunk, t), F32), pltpu.VMEM((chunk, t), F32),
                        pltpu.VMEM((chunk, t), BF16), pltpu.VMEM((chunk, t), BF16),
                        pltpu.VMEM((d, t), F32)],
        compiler_params=_cparams(("arbitrary", "arbitrary")),
        name="peer_dense",
    )(ht, u_bf, u_bf, vt3, vt3, c, e1, r2, e2, x2, mod, row(g), row(b))


def _peer_layer(x, mod, layer, wq, k1, k2, u, v, g, b, alpha):
    bn, s, d = x.shape
    n_keys, half_key = k1.shape
    n_head = wq.shape[1] // (2 * half_key)
    x2 = x.reshape(bn * s, d)
    ht, c, e1, r2, e2 = _peer_route(x2, mod, layer, s, wq.T.astype(BF16), k1.astype(BF16),
                                    k2.astype(BF16), n_head)
    out = _peer_dense(x2, mod, layer, s, ht, c, e1, r2, e2, u.astype(BF16), v.astype(BF16),
                      g, b, alpha)
    return out.reshape(bn, s, d)


def kernel(x, c, ada_mix_w, ada_mix_b, ln_mix_g, ln_mix_b, conv_in_w, conv_in_b, conv_dw_w, conv_dw_b, conv_ln_g, conv_ln_b, conv_out_w, conv_out_b, attn_in_w, attn_in_b, attn_out_w, attn_out_b, ada_ffn_w, ada_ffn_b, ln_ffn_g, ln_ffn_b, peer_query_w, peer_sub_keys_1, peer_sub_keys_2, peer_expert_u, peer_expert_v):
    depth = ada_mix_w.shape[0]
    d = x.shape[-1]
    n_attn_head = attn_in_w.shape[-1] - 3 * d
    alpha = (2 * depth) ** 0.25
    mod_mix = _adaln(c, ada_mix_w, ada_mix_b)
    mod_ffn = _adaln(c, ada_ffn_w, ada_ffn_b)
    for i in range(depth):
        j = i // 2
        if i % 2 == 0:
            x = _conv_mixer(x, mod_mix, i, conv_in_w[j], conv_in_b[j], conv_dw_w[j], conv_dw_b[j],
                            conv_ln_g[j], conv_ln_b[j], conv_out_w[j], conv_out_b[j],
                            ln_mix_g[i], ln_mix_b[i], alpha)
        else:
            q, k, v, cum = _qkv(x, mod_mix, i, attn_in_w[j], attn_in_b[j], n_attn_head)
            cum_t = jnp.swapaxes(cum[:, :, :n_attn_head], 1, 2)
            o = _attention(q, k, v, cum_t[:, :, None, :], cum_t[..., None], d // n_attn_head)
            x = _attn_out(o, x, mod_mix, i, attn_out_w[j], attn_out_b[j],
                          ln_mix_g[i], ln_mix_b[i], alpha)
        x = _peer_layer(x, mod_ffn, i, peer_query_w[i], peer_sub_keys_1[i], peer_sub_keys_2[i],
                        peer_expert_u[i], peer_expert_v[i], ln_ffn_g[i], ln_ffn_b[i], alpha)
    return x
```

```python
import functools
import math

import jax
import jax.numpy as jnp
from jax import lax
from jax.experimental import pallas as pl
from jax.experimental.pallas import tpu as pltpu

F32 = jnp.float32
BF16 = jnp.bfloat16

LN_EPS = 1e-5
CONV_HALO = 32
PEER_TOPK = 16
TOKEN_TILE = 512
ATTN_Q_TILE = 2048
ATTN_K_TILE = 512
ADALN_COL_TILE = 1024
PEER_STAGE_KEYS = 2
PEER_STAGES_PER_STEP = 8
V7X_VMEM_LIMIT_BYTES = 56 * 1024 * 1024
SUBLANES = 8
LANES = 128
INV_SQRT2 = 1.0 / math.sqrt(2.0)
NEG_INF = float("-inf")


def _cparams(sem):
    return pltpu.CompilerParams(dimension_semantics=sem, vmem_limit_bytes=V7X_VMEM_LIMIT_BYTES)


def _layernorm(z, g, b):
    mu = jnp.mean(z, axis=-1, keepdims=True)
    zc = z - mu
    var = jnp.mean(zc * zc, axis=-1, keepdims=True)
    return zc * lax.rsqrt(var + LN_EPS) * g + b


def _split_mod(mod, d):
    return mod[:, :d], mod[:, d:2 * d], mod[:, 2 * d:]


def _ada_kernel(c_ref, w_ref, b_ref, o_ref):
    c = c_ref[...]
    sc = (c * jax.nn.sigmoid(c)).astype(BF16)
    o_ref[0] = jnp.dot(sc, w_ref[0].astype(BF16), preferred_element_type=F32) + b_ref[0]


def _adaln(c, w, b):
    n_layer, d, d3 = w.shape
    bn = c.shape[0]
    tn = min(d3, ADALN_COL_TILE)
    out = pl.pallas_call(
        _ada_kernel,
        grid=(n_layer, d3 // tn),
        in_specs=[pl.BlockSpec((bn, d), lambda l, j: (0, 0)),
                  pl.BlockSpec((1, d, tn), lambda l, j: (l, 0, j)),
                  pl.BlockSpec((1, 1, tn), lambda l, j: (l, 0, j))],
        out_specs=pl.BlockSpec((1, bn, tn), lambda l, j: (l, 0, j)),
        out_shape=jax.ShapeDtypeStruct((n_layer, bn, d3), F32),
        compiler_params=_cparams(("arbitrary", "arbitrary")),
        name="adaln",
    )(c, w, b.reshape(n_layer, 1, d3))
    return out.reshape(n_layer, bn, 1, d3)


def _conv_kernel(x_ref, mod_ref, w_in_ref, b_in_ref, w_dw_ref, b_dw_ref, cg_ref, cb_ref,
                 w_out_ref, b_out_ref, g_ref, b_ref, o_ref, abuf, shbuf, *, width, alpha):
    t, d = x_ref.shape[1], x_ref.shape[2]

    @pl.when(pl.program_id(1) == 0)
    def _():
        abuf[0:CONV_HALO, :] = jnp.zeros((CONV_HALO, d), F32)

    x = x_ref[0]
    shift, scale, gate = _split_mod(mod_ref[0, 0], d)
    h = (x * (1.0 + scale) + shift).astype(BF16)
    p = jnp.dot(h, w_in_ref[...], preferred_element_type=F32) + b_in_ref[...]
    abuf[CONV_HALO:CONV_HALO + t, :] = p[:, :d] * jax.nn.sigmoid(p[:, d:])
    base = CONV_HALO - (width - 1)
    n_sh = shbuf.shape[1]
    for s in range(1, SUBLANES):
        shbuf[s - 1] = abuf[s:s + n_sh, :]
    acc = jnp.zeros((t, d), F32) + b_dw_ref[...]
    for k in range(width):
        row0, s = (base + k) // SUBLANES * SUBLANES, (base + k) % SUBLANES
        tap = abuf[row0:row0 + t, :] if s == 0 else shbuf[s - 1, row0:row0 + t, :]
        acc = acc + w_dw_ref[k:k + 1, :] * tap
    abuf[0:CONV_HALO, :] = abuf[t:t + CONV_HALO, :]
    a = _layernorm(acc, cg_ref[...], cb_ref[...])
    a = (a * jax.nn.sigmoid(a)).astype(BF16)
    y = jnp.dot(a, w_out_ref[...], preferred_element_type=F32) + b_out_ref[...]
    o_ref[0] = _layernorm(alpha * x + gate * y, g_ref[...], b_ref[...])


def _conv_mixer(x, mod, layer, w_in, b_in, w_dw, b_dw, cg, cb, w_out, b_out, g, b, alpha):
    bn, s, d = x.shape
    t = min(s, TOKEN_TILE)
    width = w_dw.shape[0]
    row = lambda v: v.reshape(1, -1)
    const = lambda shape: pl.BlockSpec(shape, lambda i, j: (0,) * len(shape))
    return pl.pallas_call(
        functools.partial(_conv_kernel, width=width, alpha=alpha),
        grid=(bn, s // t),
        in_specs=[pl.BlockSpec((1, t, d), lambda i, j: (i, j, 0)),
                  pl.BlockSpec((1, 1, 1, 3 * d), lambda i, j: (layer, i, 0, 0)),
                  const((d, 2 * d)), const((1, 2 * d)), const((width, d)), const((1, d)),
                  const((1, d)), const((1, d)), const((d, d)), const((1, d)),
                  const((1, d)), const((1, d))],
        out_specs=pl.BlockSpec((1, t, d), lambda i, j: (i, j, 0)),
        out_shape=jax.ShapeDtypeStruct((bn, s, d), F32),
        scratch_shapes=[pltpu.VMEM((t + CONV_HALO, d), F32),
                        pltpu.VMEM((SUBLANES - 1, t + CONV_HALO - SUBLANES, d), F32)],
        compiler_params=_cparams(("arbitrary", "arbitrary")),
        name="conv_mixer",
    )(x, mod, w_in.astype(BF16), row(b_in), w_dw, row(b_dw), row(cg), row(cb),
      w_out.astype(BF16), row(b_out), row(g), row(b))


def _qkv_kernel(x_ref, mod_ref, w_ref, b_ref, wf_ref, bf_ref, q_ref, k_ref, v_ref, cum_ref, carry,
                *, q_scale):
    t, d = x_ref.shape[1], x_ref.shape[2]

    @pl.when(pl.program_id(1) == 0)
    def _():
        carry[...] = jnp.zeros(carry.shape, F32)

    shift, scale, _ = _split_mod(mod_ref[0, 0], d)
    h = (x_ref[0] * (1.0 + scale) + shift).astype(BF16)
    proj = jnp.dot(h, w_ref[...], preferred_element_type=F32) + b_ref[...]
    q_ref[0] = (proj[:, :d] * q_scale).astype(BF16)
    k_ref[0] = proj[:, d:2 * d].astype(BF16)
    v_ref[0] = proj[:, 2 * d:].astype(BF16)
    f = jnp.dot(h, wf_ref[...], preferred_element_type=F32) + bf_ref[...]
    log_f = jax.nn.log_sigmoid(f)
    r = lax.broadcasted_iota(jnp.int32, (t, t), 0)
    c = lax.broadcasted_iota(jnp.int32, (t, t), 1)
    tri = jnp.where(c <= r, 1.0, 0.0).astype(BF16)
    hi = log_f.astype(BF16)
    rem = log_f - hi.astype(F32)
    mid = rem.astype(BF16)
    lo = (rem - mid.astype(F32)).astype(BF16)
    cum = (jnp.dot(tri, hi, preferred_element_type=F32)
           + jnp.dot(tri, mid, preferred_element_type=F32)
           + jnp.dot(tri, lo, preferred_element_type=F32)) + carry[0:1, :]
    cum_ref[0] = cum
    carry[0:1, :] = cum[t - 1:t, :]


def _qkv(x, mod, layer, w_in, b_in, n_head):
    bn, s, d = x.shape
    hd = d // n_head
    t = min(s, TOKEN_TILE)
    w_qkv = w_in[:, :3 * d].astype(BF16)
    b_qkv = b_in[:3 * d].reshape(1, 3 * d)
    w_f = jnp.pad(w_in[:, 3 * d:], ((0, 0), (0, LANES - n_head))).astype(BF16)
    b_f = jnp.pad(b_in[3 * d:], (0, LANES - n_head)).reshape(1, LANES)
    const = lambda shape: pl.BlockSpec(shape, lambda i, j: (0,) * len(shape))
    head_spec = pl.BlockSpec((1, t, d), lambda i, j: (i, j, 0))
    head_shape = jax.ShapeDtypeStruct((bn, s, d), BF16)
    return pl.pallas_call(
        functools.partial(_qkv_kernel, q_scale=hd ** -0.5),
        grid=(bn, s // t),
        in_specs=[pl.BlockSpec((1, t, d), lambda i, j: (i, j, 0)),
                  pl.BlockSpec((1, 1, 1, 3 * d), lambda i, j: (layer, i, 0, 0)),
                  const((d, 3 * d)), const((1, 3 * d)), const((d, LANES)), const((1, LANES))],
        out_specs=[head_spec, head_spec, head_spec,
                   pl.BlockSpec((1, t, LANES), lambda i, j: (i, j, 0))],
        out_shape=[head_shape, head_shape, head_shape,
                   jax.ShapeDtypeStruct((bn, s, LANES), F32)],
        scratch_shapes=[pltpu.VMEM((SUBLANES, LANES), F32)],
        compiler_params=_cparams(("arbitrary", "arbitrary")),
        name="fox_qkv",
    )(x, mod, w_qkv, b_qkv, w_f, b_f)


def _attn_kernel(q_ref, k_ref, v_ref, fq_ref, fk_ref, o_ref, vt_aug, *, tk, hd):
    tq, bw = q_ref.shape[1], q_ref.shape[2]
    n_sub = bw // hd
    n_aug = vt_aug.shape[2]
    qi = pl.program_id(2)

    @pl.when(qi == 0)
    def _():
        row = lax.broadcasted_iota(jnp.int32, (n_aug - hd, tk), 0)
        ones_row = jnp.where(row == 0, 1.0, 0.0).astype(vt_aug.dtype)
        for kv in range(vt_aug.shape[0]):
            v_t = v_ref[0, kv * tk:(kv + 1) * tk, :].astype(F32).T.astype(vt_aug.dtype)
            for j in range(n_sub):
                vt_aug[kv, j, 0:hd, :] = v_t[j * hd:(j + 1) * hd, :]
                vt_aug[kv, j, hd:n_aug, :] = ones_row

    q2 = q_ref[0]
    lane = lax.broadcasted_iota(jnp.int32, (tq, bw), 1)
    qh = [jnp.where((lane >= j * hd) & (lane < (j + 1) * hd), q2, jnp.zeros((), q2.dtype))
          for j in range(n_sub)]
    fq = [fq_ref[0, j] for j in range(n_sub)]

    def step(kv, carry, masked, q_lo=0):
        off = pl.multiple_of(kv * tk, tk)
        k = k_ref[0, pl.ds(off, tk), :]
        nq = tq - q_lo
        out = []
        for j in range(n_sub):
            m_all, acc_all = carry[j]
            m, acc, fq_j = m_all[:, q_lo:], acc_all[:, q_lo:], fq[j][:, q_lo:]
            z = lax.dot_general(k, qh[j][q_lo:, :], (((1,), (1,)), ((), ())),
                                preferred_element_type=F32)
            z = z - fk_ref[0, j, pl.ds(off, tk), :]
            if masked:
                key = off + lax.broadcasted_iota(jnp.int32, (tk, tk), 0)
                qry = qi * tq + q_lo + lax.broadcasted_iota(jnp.int32, (tk, tk), 1)
                z_diag = jnp.where(key <= qry, z[:, :tk], NEG_INF)
                z = z_diag if nq == tk else jnp.concatenate([z_diag, z[:, tk:]], axis=1)
            m_new = jnp.maximum(m, jnp.max(z, axis=0, keepdims=True) + fq_j)
            a = jnp.exp(m - m_new)
            p = jnp.exp((z - (m_new - fq_j)).astype(BF16))
            acc = a * acc + jnp.dot(vt_aug[kv, j], p, preferred_element_type=F32)
            if q_lo:
                m_new = jnp.concatenate([m_all[:, :q_lo], m_new], axis=1)
                acc = jnp.concatenate([acc_all[:, :q_lo], acc], axis=1)
            out.append((m_new, acc))
        return tuple(out)

    carry = tuple((jnp.full((1, tq), NEG_INF, F32), jnp.zeros((n_aug, tq), F32))
                  for _ in range(n_sub))
    n_diag = tq // tk
    carry = lax.fori_loop(0, qi * n_diag, lambda kv, c: step(kv, c, False), carry)
    for i in range(n_diag):
        carry = step(qi * n_diag + i, carry, True, q_lo=i * tk)
    res = carry
    o_t = jnp.concatenate([acc[:hd, :] / acc[hd:hd + 1, :] for _, acc in res], axis=0)
    o_ref[0] = o_t.T.astype(o_ref.dtype)


def _attention(q, k, v, fq, fk, hd):
    bn, s, d = q.shape
    bw = LANES
    n_sub = bw // hd
    tk = min(s, ATTN_K_TILE)
    tq = min(s, ATTN_Q_TILE)
    return pl.pallas_call(
        functools.partial(_attn_kernel, tk=tk, hd=hd),
        grid=(bn, d // bw, s // tq),
        in_specs=[pl.BlockSpec((1, tq, bw), lambda b, h, i: (b, i, h)),
                  pl.BlockSpec((1, s, bw), lambda b, h, i: (b, 0, h)),
                  pl.BlockSpec((1, s, bw), lambda b, h, i: (b, 0, h)),
                  pl.BlockSpec((1, n_sub, 1, tq), lambda b, h, i: (b, h, 0, i)),
                  pl.BlockSpec((1, n_sub, s, 1), lambda b, h, i: (b, h, 0, 0))],
        out_specs=pl.BlockSpec((1, tq, bw), lambda b, h, i: (b, i, h)),
        out_shape=jax.ShapeDtypeStruct((bn, s, d), BF16),
        scratch_shapes=[pltpu.VMEM((s // tk, n_sub, hd + 2 * SUBLANES, tk), BF16)],
        compiler_params=_cparams(("arbitrary", "arbitrary", "arbitrary")),
        name="fox_attention",
    )(q, k, v, fq, fk)


def _attn_out_kernel(o_ref, x_ref, mod_ref, w_ref, b_ref, g_ref, bb_ref, out_ref, *, alpha):
    d = x_ref.shape[2]
    y = jnp.dot(o_ref[0], w_ref[...], preferred_element_type=F32) + b_ref[...]
    _, _, gate = _split_mod(mod_ref[0, 0], d)
    out_ref[0] = _layernorm(alpha * x_ref[0] + gate * y, g_ref[...], bb_ref[...])


def _attn_out(o, x, mod, layer, w_out, b_out, g, b, alpha):
    bn, s, d = x.shape
    t = min(s, TOKEN_TILE)
    row = lambda v: v.reshape(1, -1)
    const = lambda shape: pl.BlockSpec(shape, lambda i, j: (0,) * len(shape))
    return pl.pallas_call(
        functools.partial(_attn_out_kernel, alpha=alpha),
        grid=(bn, s // t),
        in_specs=[pl.BlockSpec((1, t, d), lambda i, j: (i, j, 0)),
                  pl.BlockSpec((1, t, d), lambda i, j: (i, j, 0)),
                  pl.BlockSpec((1, 1, 1, 3 * d), lambda i, j: (layer, i, 0, 0)),
                  const((d, d)), const((1, d)), const((1, d)), const((1, d))],
        out_specs=pl.BlockSpec((1, t, d), lambda i, j: (i, j, 0)),
        out_shape=jax.ShapeDtypeStruct((bn, s, d), F32),
        compiler_params=_cparams(("arbitrary", "arbitrary")),
        name="fox_out",
    )(o, x, mod, w_out.astype(BF16), row(b_out), row(g), row(b))


def _sort_desc(a):
    a = list(a)
    n = len(a)
    k = 2
    while k <= n:
        j = k // 2
        while j >= 1:
            for i in range(n):
                l = i ^ j
                if l > i:
                    hi, lo = jnp.maximum(a[i], a[l]), jnp.minimum(a[i], a[l])
                    a[i], a[l] = (hi, lo) if (i & k) == 0 else (lo, hi)
            j //= 2
        k *= 2
    return a


def _bitonic_merge_desc(t):
    t = list(t)
    n = len(t)
    j = n // 2
    while j >= 1:
        for i in range(n):
            l = i ^ j
            if l > i:
                t[i], t[l] = jnp.maximum(t[i], t[l]), jnp.minimum(t[i], t[l])
        j //= 2
    return t


def _merge_top(a, b):
    n = len(a)
    t = [a[i] if b[n - 1 - i] is None else jnp.maximum(a[i], b[n - 1 - i]) for i in range(n)]
    return _bitonic_merge_desc(t)


def _select_by_bits(table, bits):
    for b in bits:
        table = [jnp.where(b, table[2 * i + 1], table[2 * i]) for i in range(len(table) // 2)]
    return table[0]


def _rank_bits(v, s):
    n = len(v)
    known = []
    step = n // 2
    while step >= 1:
        pivots = [v[lo + step - 1] for lo in range(0, n, 2 * step)]
        known.append(_select_by_bits(pivots, known[::-1]) > s)
        step //= 2
    return known[::-1], v[n - 1] > s


def _top_values(s_ref, tk):
    n_keys = s_ref.shape[0]
    groups = [s_ref[g * SUBLANES:(g + 1) * SUBLANES, :] for g in range(n_keys // SUBLANES)]
    top = _sort_desc(groups[:tk])
    for g0 in range(tk, len(groups), tk):
        top = _merge_top(top, _sort_desc(groups[g0:g0 + tk]))
    shift = SUBLANES // 2
    while shift >= 1:
        top = _merge_top(top, [pltpu.roll(v, shift, 0) for v in top])
        shift //= 2
    return top


def _candidate_counts(v1, v2, tk):
    cand = {}
    for i in range(tk):
        for j in range(tk // (i + 1)):
            cand[i, j] = v1[i] + v2[j]
    half = tk // 2
    top = [cand[0, j] for j in range(tk)]
    lst_a = [cand[1, j] for j in range(half)] + [cand[i, 0] for i in range(tk - 1, half - 1, -1)]
    top = _merge_top(top, _bitonic_merge_desc(lst_a))
    rest = [cand[i, j] for i in range(2, half) for j in range(tk // (i + 1))]
    for g0 in range(0, len(rest), tk):
        grp = rest[g0:g0 + tk]
        if len(grp) == tk:
            grp = _sort_desc(grp)
        else:
            grp = _sort_desc(grp + [jnp.full_like(grp[0], NEG_INF)] * (tk - len(grp)))
        top = _merge_top(top, grp)
    tau = top[tk - 1]
    e1 = [jnp.exp(v1[i] - v1[0]) for i in range(tk)]
    e2 = [jnp.exp(v2[j] - v2[0]) for j in range(tk)]
    cnt, z = [], None
    for i in range(tk):
        c_i, z_i = None, None
        for j in range(tk // (i + 1)):
            sel = cand[i, j] >= tau
            one = jnp.where(sel, 1.0, 0.0)
            w = jnp.where(sel, e2[j], 0.0)
            c_i = one if c_i is None else c_i + one
            z_i = w if z_i is None else z_i + w
        cnt.append(c_i)
        z = e1[i] * z_i if z is None else z + e1[i] * z_i
    return cnt, 1.0 / z


def _route_kernel(x_ref, mod_ref, wq_ref, k1_ref, k2_ref,
                  ht_ref, c_ref, e1_ref, r2_ref, e2_ref,
                  q_scr, s_scr, v_scr, vh_scr, cs_scr, *, n_head, tk):
    t, d = x_ref.shape
    n_keys = k1_ref.shape[0]
    shift, scale, _ = _split_mod(mod_ref[0, 0], d)
    h = x_ref[...] * (1.0 + scale) + shift
    ht = h.T.astype(BF16)
    ht_ref[...] = ht
    q_scr[...] = jnp.dot(wq_ref[...], ht, preferred_element_type=F32).astype(BF16)

    def scores_and_top(hh, carry):
        for half, k_ref in enumerate((k1_ref, k2_ref)):
            off = pl.multiple_of((2 * hh + half) * n_keys, n_keys)
            s_scr[hh, half] = jnp.dot(k_ref[...], q_scr[pl.ds(off, n_keys), :],
                                      preferred_element_type=F32)
            top = _top_values(s_scr.at[hh, half], tk)
            for i in range(tk):
                v_scr[hh, half, i] = top[i]
                vh_scr[half, i, pl.ds(hh, 1), :] = top[i][0:1, :]
        return carry

    lax.fori_loop(0, n_head, scores_and_top, 0)

    v1 = [vh_scr[0, i] for i in range(tk)]
    v2 = [vh_scr[1, i] for i in range(tk)]
    cnt, inv_z = _candidate_counts(v1, v2, tk)
    for i in range(tk):
        cs_scr[i] = cnt[i]
    cs_scr[tk] = inv_z

    def dense(hh, carry):
        def row(idx):
            return jnp.broadcast_to(cs_scr[idx, pl.ds(hh, 1), :], (SUBLANES, t))
        cnt_h = [row(i) for i in range(tk)]
        inv = row(tk)
        v1h = [v_scr[hh, 0, i] for i in range(tk)]
        v2h = [v_scr[hh, 1, i] for i in range(tk)]
        for g in range(n_keys // SUBLANES):
            rows = slice(g * SUBLANES, (g + 1) * SUBLANES)
            s1 = s_scr[hh, 0, rows, :]
            s2 = s_scr[hh, 1, rows, :]
            bits1, below1 = _rank_bits(v1h, s1)
            bits2, below2 = _rank_bits(v2h, s2)
            c = jnp.where(below1, 0.0, _select_by_bits(cnt_h, bits1))
            r2 = jnp.where(below2, 1.0, 0.0)
            for b, weight in zip(bits2, (1.0, 2.0, 4.0, 8.0)):
                r2 = r2 + jnp.where(b, weight, 0.0)
            c_ref[hh, rows, :] = c
            e1_ref[hh, rows, :] = jnp.exp(s1 - v1h[0])
            r2_ref[hh, rows, :] = r2.astype(BF16)
            e2_ref[hh, rows, :] = (jnp.exp(s2 - v2h[0]) * inv).astype(BF16)
        return carry

    lax.fori_loop(0, n_head, dense, 0)


def _peer_route(x2, mod, layer, seq, wq_t, k1, k2, n_head):
    n_tok, d = x2.shape
    n_keys = k1.shape[0]
    t = min(n_tok, TOKEN_TILE)
    tk = PEER_TOPK
    const = lambda shape: pl.BlockSpec(shape, lambda i: (0,) * len(shape))
    tok3 = pl.BlockSpec((n_head, n_keys, t), lambda i: (0, 0, i))
    return pl.pallas_call(
        functools.partial(_route_kernel, n_head=n_head, tk=tk),
        grid=(n_tok // t,),
        in_specs=[pl.BlockSpec((t, d), lambda i: (i, 0)),
                  pl.BlockSpec((1, 1, 1, 3 * d), lambda i: (layer, (i * t) // seq, 0, 0)),
                  const(wq_t.shape), const(k1.shape), const(k2.shape)],
        out_specs=[pl.BlockSpec((d, t), lambda i: (0, i)), tok3, tok3, tok3, tok3],
        out_shape=[jax.ShapeDtypeStruct((d, n_tok), BF16),
                   jax.ShapeDtypeStruct((n_head, n_keys, n_tok), F32),
                   jax.ShapeDtypeStruct((n_head, n_keys, n_tok), F32),
                   jax.ShapeDtypeStruct((n_head, n_keys, n_tok), BF16),
                   jax.ShapeDtypeStruct((n_head, n_keys, n_tok), BF16)],
        scratch_shapes=[pltpu.VMEM((2 * n_head * n_keys, t), BF16),
                        pltpu.VMEM((n_head, 2, n_keys, t), F32),
                        pltpu.VMEM((n_head, 2, tk, SUBLANES, t), F32),
                        pltpu.VMEM((2, tk, SUBLANES, t), F32),
                        pltpu.VMEM((tk + 1, SUBLANES, t), F32)],
        compiler_params=_cparams(("arbitrary",)),
        name="peer_route",
    )(x2, mod, wq_t, k1, k2)


def _peer_dense_kernel(ht_ref, u_ref, un_ref, vt_ref, vp_ref, c_ref, e1_ref, r2_ref, e2_ref,
                       x_ref, mod_ref, g_ref, b_ref, o_ref, a0, a1, w0, w1, acc_scr,
                       *, n_head, n_keys, alpha):
    t, d = x_ref.shape
    chunk = a0.shape[0]
    n_c = u_ref.shape[0] // chunk
    n_a = chunk // n_keys
    assert n_c % 2 == 0
    pack = 2 * SUBLANES
    k = pl.program_id(1)
    a_buf, w_buf = (a0, a1), (w0, w1)

    def u_matmul(u_rows, dst):
        dst[...] = jnp.dot(u_rows, ht_ref[...], preferred_element_type=F32)

    def v_matmul(vt_chunk, w_scr):
        acc_scr[...] += jnp.dot(vt_chunk, w_scr[...], preferred_element_type=F32)

    @pl.when(k == 0)
    def _():
        w1[...] = jnp.zeros(w1.shape, w1.dtype)
        acc_scr[...] = jnp.zeros(acc_scr.shape, F32)
        u_matmul(u_ref[0:chunk, :], a0)

    def gate_pass(a_scr, w_scr, c):
        for a in range(n_a):
            key = (k * n_c + c) * n_a + a
            pre = a_scr[a * n_keys:(a + 1) * n_keys, :]
            act = (0.5 * pre * (1.0 + lax.erf(pre * INV_SQRT2))).astype(BF16)
            gate = [jnp.zeros((pack, t), BF16) for _ in range(n_keys // pack)]
            for hh in range(n_head):
                cnt = jnp.broadcast_to(c_ref[hh, pl.ds(key, 1), :], (pack, t)).astype(BF16)
                e1 = jnp.broadcast_to(e1_ref[hh, pl.ds(key, 1), :], (pack, t)).astype(BF16)
                for bc in range(n_keys // pack):
                    rows = slice(bc * pack, (bc + 1) * pack)
                    sel = jnp.where(r2_ref[hh, rows, :] < cnt, e2_ref[hh, rows, :],
                                    jnp.zeros((), BF16))
                    gate[bc] = gate[bc] + sel * e1
            for bc in range(n_keys // pack):
                r0 = a * n_keys + bc * pack
                w_scr[r0:r0 + pack, :] = act[bc * pack:(bc + 1) * pack, :] * gate[bc]

    for c in range(n_c):
        u_next = un_ref[...] if c == n_c - 1 else u_ref[(c + 1) * chunk:(c + 2) * chunk, :]
        u_matmul(u_next, a_buf[(c + 1) % 2])
        gate_pass(a_buf[c % 2], w_buf[c % 2], c)
        v_matmul(vp_ref[0] if c == 0 else vt_ref[c - 1], w_buf[(c - 1) % 2])

    @pl.when(k == pl.num_programs(1) - 1)
    def _():
        v_matmul(vt_ref[n_c - 1], w_buf[(n_c - 1) % 2])
        _, _, gate_mod = _split_mod(mod_ref[0, 0], d)
        y = acc_scr[...].T
        o_ref[...] = _layernorm(alpha * x_ref[...] + gate_mod * y, g_ref[...], b_ref[...])


def _peer_dense(x2, mod, layer, seq, ht, c, e1, r2, e2, u_bf, v_bf, g, b, alpha):
    n_tok, d = x2.shape
    n_head, n_keys, _ = c.shape
    n_exp = u_bf.shape[0]
    t = min(n_tok, TOKEN_TILE)
    chunk = PEER_STAGE_KEYS * n_keys
    n_c = PEER_STAGES_PER_STEP
    n_chunk = n_exp // chunk
    n_k = n_chunk // n_c
    vt3 = v_bf.reshape(n_chunk, chunk, d).transpose(0, 2, 1)
    tok3 = pl.BlockSpec((n_head, n_keys, t), lambda i, j: (0, 0, i))
    row = lambda v: v.reshape(1, -1)
    return pl.pallas_call(
        functools.partial(_peer_dense_kernel, n_head=n_head, n_keys=n_keys, alpha=alpha),
        grid=(n_tok // t, n_k),
        in_specs=[pl.BlockSpec((d, t), lambda i, j: (0, i)),
                  pl.BlockSpec((n_c * chunk, d), lambda i, j: (j, 0)),
                  pl.BlockSpec((chunk, d), lambda i, j: (jnp.minimum((j + 1) * n_c, n_chunk - 1), 0)),
                  pl.BlockSpec((n_c, d, chunk), lambda i, j: (j, 0, 0)),
                  pl.BlockSpec((1, d, chunk), lambda i, j: (jnp.maximum(j * n_c - 1, 0), 0, 0)),
                  tok3, tok3, tok3, tok3,
                  pl.BlockSpec((t, d), lambda i, j: (i, 0)),
                  pl.BlockSpec((1, 1, 1, 3 * d), lambda i, j: (layer, (i * t) // seq, 0, 0)),
                  pl.BlockSpec((1, d), lambda i, j: (0, 0)),
                  pl.BlockSpec((1, d), lambda i, j: (0, 0))],
        out_specs=pl.BlockSpec((t, d), lambda i, j: (i, 0)),
        out_shape=jax.ShapeDtypeStruct((n_tok, d), F32),
        scratch_shapes=[pltpu.VMEM((chunk, t), F32), pltpu.VMEM((chunk, t), F32),
                        pltpu.VMEM((chunk, t), BF16), pltpu.VMEM((chunk, t), BF16),
                        pltpu.VMEM((d, t), F32)],
        compiler_params=_cparams(("arbitrary", "arbitrary")),
        name="peer_dense",
    )(ht, u_bf, u_bf, vt3, vt3, c, e1, r2, e2, x2, mod, row(g), row(b))


def _peer_layer(x, mod, layer, wq, k1, k2, u, v, g, b, alpha):
    bn, s, d = x.shape
    n_keys, half_key = k1.shape
    n_head = wq.shape[1] // (2 * half_key)
    x2 = x.reshape(bn * s, d)
    ht, c, e1, r2, e2 = _peer_route(x2, mod, layer, s, wq.T.astype(BF16), k1.astype(BF16),
                                    k2.astype(BF16), n_head)
    out = _peer_dense(x2, mod, layer, s, ht, c, e1, r2, e2, u.astype(BF16), v.astype(BF16),
                      g, b, alpha)
    return out.reshape(bn, s, d)


def kernel(x, c, ada_mix_w, ada_mix_b, ln_mix_g, ln_mix_b, conv_in_w, conv_in_b, conv_dw_w, conv_dw_b, conv_ln_g, conv_ln_b, conv_out_w, conv_out_b, attn_in_w, attn_in_b, attn_out_w, attn_out_b, ada_ffn_w, ada_ffn_b, ln_ffn_g, ln_ffn_b, peer_query_w, peer_sub_keys_1, peer_sub_keys_2, peer_expert_u, peer_expert_v):
    depth = ada_mix_w.shape[0]
    d = x.shape[-1]
    n_attn_head = attn_in_w.shape[-1] - 3 * d
    alpha = (2 * depth) ** 0.25
    mod_mix = _adaln(c, ada_mix_w, ada_mix_b)
    mod_ffn = _adaln(c, ada_ffn_w, ada_ffn_b)
    for i in range(depth):
        j = i // 2
        if i % 2 == 0:
            x = _conv_mixer(x, mod_mix, i, conv_in_w[j], conv_in_b[j], conv_dw_w[j], conv_dw_b[j],
                            conv_ln_g[j], conv_ln_b[j], conv_out_w[j], conv_out_b[j],
                            ln_mix_g[i], ln_mix_b[i], alpha)
        else:
            q, k, v, cum = _qkv(x, mod_mix, i, attn_in_w[j], attn_in_b[j], n_attn_head)
            cum_t = jnp.swapaxes(cum[:, :, :n_attn_head], 1, 2)
            o = _attention(q, k, v, cum_t[:, :, None, :], cum_t[..., None], d // n_attn_head)
            x = _attn_out(o, x, mod_mix, i, attn_out_w[j], attn_out_b[j],
                          ln_mix_g[i], ln_mix_b[i], alpha)
        x = _peer_layer(x, mod_ffn, i, peer_query_w[i], peer_sub_keys_1[i], peer_sub_keys_2[i],
                        peer_expert_u[i], peer_expert_v[i], ln_ffn_g[i], ln_ffn_b[i], alpha)
    return x
```

```python
import functools
import math

import jax
import jax.numpy as jnp
from jax import lax
from jax.experimental import pallas as pl
from jax.experimental.pallas import tpu as pltpu

F32 = jnp.float32
BF16 = jnp.bfloat16

LN_EPS = 1e-5
CONV_HALO = 32
PEER_TOPK = 16
TOKEN_TILE = 512
ATTN_Q_TILE = 4096
ATTN_K_TILE = 512
ADALN_COL_TILE = 1024
PEER_STAGE_KEYS = 2
PEER_STAGES_PER_STEP = 8
V7X_VMEM_LIMIT_BYTES = 56 * 1024 * 1024
SUBLANES = 8
LANES = 128
INV_SQRT2 = 1.0 / math.sqrt(2.0)
NEG_INF = float("-inf")


def _cparams(sem):
    return pltpu.CompilerParams(dimension_semantics=sem, vmem_limit_bytes=V7X_VMEM_LIMIT_BYTES)


def _layernorm(z, g, b):
    mu = jnp.mean(z, axis=-1, keepdims=True)
    zc = z - mu
    var = jnp.mean(zc * zc, axis=-1, keepdims=True)
    return zc * lax.rsqrt(var + LN_EPS) * g + b


def _split_mod(mod, d):
    return mod[:, :d], mod[:, d:2 * d], mod[:, 2 * d:]


def _ada_kernel(c_ref, w_ref, b_ref, o_ref):
    c = c_ref[...]
    sc = (c * jax.nn.sigmoid(c)).astype(BF16)
    o_ref[0] = jnp.dot(sc, w_ref[0].astype(BF16), preferred_element_type=F32) + b_ref[0]


def _adaln(c, w, b):
    n_layer, d, d3 = w.shape
    bn = c.shape[0]
    tn = min(d3, ADALN_COL_TILE)
    out = pl.pallas_call(
        _ada_kernel,
        grid=(n_layer, d3 // tn),
        in_specs=[pl.BlockSpec((bn, d), lambda l, j: (0, 0)),
                  pl.BlockSpec((1, d, tn), lambda l, j: (l, 0, j)),
                  pl.BlockSpec((1, 1, tn), lambda l, j: (l, 0, j))],
        out_specs=pl.BlockSpec((1, bn, tn), lambda l, j: (l, 0, j)),
        out_shape=jax.ShapeDtypeStruct((n_layer, bn, d3), F32),
        compiler_params=_cparams(("arbitrary", "arbitrary")),
        name="adaln",
    )(c, w, b.reshape(n_layer, 1, d3))
    return out.reshape(n_layer, bn, 1, d3)


def _conv_kernel(x_ref, mod_ref, w_in_ref, b_in_ref, w_dw_ref, b_dw_ref, cg_ref, cb_ref,
                 w_out_ref, b_out_ref, g_ref, b_ref, o_ref, abuf, shbuf, *, width, alpha):
    t, d = x_ref.shape[1], x_ref.shape[2]

    @pl.when(pl.program_id(1) == 0)
    def _():
        abuf[0:CONV_HALO, :] = jnp.zeros((CONV_HALO, d), F32)

    x = x_ref[0]
    shift, scale, gate = _split_mod(mod_ref[0, 0], d)
    h = (x * (1.0 + scale) + shift).astype(BF16)
    p = jnp.dot(h, w_in_ref[...], preferred_element_type=F32) + b_in_ref[...]
    abuf[CONV_HALO:CONV_HALO + t, :] = p[:, :d] * jax.nn.sigmoid(p[:, d:])
    base = CONV_HALO - (width - 1)
    n_sh = shbuf.shape[1]
    for s in range(1, SUBLANES):
        shbuf[s - 1] = abuf[s:s + n_sh, :]
    acc = jnp.zeros((t, d), F32) + b_dw_ref[...]
    for k in range(width):
        row0, s = (base + k) // SUBLANES * SUBLANES, (base + k) % SUBLANES
        tap = abuf[row0:row0 + t, :] if s == 0 else shbuf[s - 1, row0:row0 + t, :]
        acc = acc + w_dw_ref[k:k + 1, :] * tap
    abuf[0:CONV_HALO, :] = abuf[t:t + CONV_HALO, :]
    a = _layernorm(acc, cg_ref[...], cb_ref[...])
    a = (a * jax.nn.sigmoid(a)).astype(BF16)
    y = jnp.dot(a, w_out_ref[...], preferred_element_type=F32) + b_out_ref[...]
    o_ref[0] = _layernorm(alpha * x + gate * y, g_ref[...], b_ref[...])


def _conv_mixer(x, mod, layer, w_in, b_in, w_dw, b_dw, cg, cb, w_out, b_out, g, b, alpha):
    bn, s, d = x.shape
    t = min(s, TOKEN_TILE)
    width = w_dw.shape[0]
    row = lambda v: v.reshape(1, -1)
    const = lambda shape: pl.BlockSpec(shape, lambda i, j: (0,) * len(shape))
    return pl.pallas_call(
        functools.partial(_conv_kernel, width=width, alpha=alpha),
        grid=(bn, s // t),
        in_specs=[pl.BlockSpec((1, t, d), lambda i, j: (i, j, 0)),
                  pl.BlockSpec((1, 1, 1, 3 * d), lambda i, j: (layer, i, 0, 0)),
                  const((d, 2 * d)), const((1, 2 * d)), const((width, d)), const((1, d)),
                  const((1, d)), const((1, d)), const((d, d)), const((1, d)),
                  const((1, d)), const((1, d))],
        out_specs=pl.BlockSpec((1, t, d), lambda i, j: (i, j, 0)),
        out_shape=jax.ShapeDtypeStruct((bn, s, d), F32),
        scratch_shapes=[pltpu.VMEM((t + CONV_HALO, d), F32),
                        pltpu.VMEM((SUBLANES - 1, t + CONV_HALO - SUBLANES, d), F32)],
        compiler_params=_cparams(("arbitrary", "arbitrary")),
        name="conv_mixer",
    )(x, mod, w_in.astype(BF16), row(b_in), w_dw, row(b_dw), row(cg), row(cb),
      w_out.astype(BF16), row(b_out), row(g), row(b))


def _qkv_kernel(x_ref, mod_ref, w_ref, b_ref, wf_ref, bf_ref, q_ref, k_ref, v_ref, cum_ref, carry,
                *, q_scale):
    t, d = x_ref.shape[1], x_ref.shape[2]

    @pl.when(pl.program_id(1) == 0)
    def _():
        carry[...] = jnp.zeros(carry.shape, F32)

    shift, scale, _ = _split_mod(mod_ref[0, 0], d)
    h = (x_ref[0] * (1.0 + scale) + shift).astype(BF16)
    proj = jnp.dot(h, w_ref[...], preferred_element_type=F32) + b_ref[...]
    q_ref[0] = (proj[:, :d] * q_scale).astype(BF16)
    k_ref[0] = proj[:, d:2 * d].astype(BF16)
    v_ref[0] = proj[:, 2 * d:].astype(BF16)
    f = jnp.dot(h, wf_ref[...], preferred_element_type=F32) + bf_ref[...]
    log_f = jax.nn.log_sigmoid(f)
    r = lax.broadcasted_iota(jnp.int32, (t, t), 0)
    c = lax.broadcasted_iota(jnp.int32, (t, t), 1)
    tri = jnp.where(c <= r, 1.0, 0.0).astype(BF16)
    hi = log_f.astype(BF16)
    rem = log_f - hi.astype(F32)
    mid = rem.astype(BF16)
    lo = (rem - mid.astype(F32)).astype(BF16)
    cum = (jnp.dot(tri, hi, preferred_element_type=F32)
           + jnp.dot(tri, mid, preferred_element_type=F32)
           + jnp.dot(tri, lo, preferred_element_type=F32)) + carry[0:1, :]
    cum_ref[0] = cum
    carry[0:1, :] = cum[t - 1:t, :]


def _qkv(x, mod, layer, w_in, b_in, n_head):
    bn, s, d = x.shape
    hd = d // n_head
    t = min(s, TOKEN_TILE)
    w_qkv = w_in[:, :3 * d].astype(BF16)
    b_qkv = b_in[:3 * d].reshape(1, 3 * d)
    w_f = jnp.pad(w_in[:, 3 * d:], ((0, 0), (0, LANES - n_head))).astype(BF16)
    b_f = jnp.pad(b_in[3 * d:], (0, LANES - n_head)).reshape(1, LANES)
    const = lambda shape: pl.BlockSpec(shape, lambda i, j: (0,) * len(shape))
    head_spec = pl.BlockSpec((1, t, d), lambda i, j: (i, j, 0))
    head_shape = jax.ShapeDtypeStruct((bn, s, d), BF16)
    return pl.pallas_call(
        functools.partial(_qkv_kernel, q_scale=hd ** -0.5),
        grid=(bn, s // t),
        in_specs=[pl.BlockSpec((1, t, d), lambda i, j: (i, j, 0)),
                  pl.BlockSpec((1, 1, 1, 3 * d), lambda i, j: (layer, i, 0, 0)),
                  const((d, 3 * d)), const((1, 3 * d)), const((d, LANES)), const((1, LANES))],
        out_specs=[head_spec, head_spec, head_spec,
                   pl.BlockSpec((1, t, LANES), lambda i, j: (i, j, 0))],
        out_shape=[head_shape, head_shape, head_shape,
                   jax.ShapeDtypeStruct((bn, s, LANES), F32)],
        scratch_shapes=[pltpu.VMEM((SUBLANES, LANES), F32)],
        compiler_params=_cparams(("arbitrary", "arbitrary")),
        name="fox_qkv",
    )(x, mod, w_qkv, b_qkv, w_f, b_f)


def _attn_kernel(q_ref, k_ref, v_ref, fq_ref, fk_ref, o_ref, vt_aug, *, tk, hd):
    tq, bw = q_ref.shape[1], q_ref.shape[2]
    n_sub = bw // hd
    n_aug = vt_aug.shape[2]
    qi = pl.program_id(2)

    @pl.when(qi == 0)
    def _():
        row = lax.broadcasted_iota(jnp.int32, (n_aug - hd, tk), 0)
        ones_row = jnp.where(row == 0, 1.0, 0.0).astype(vt_aug.dtype)
        for kv in range(vt_aug.shape[0]):
            v_t = v_ref[0, kv * tk:(kv + 1) * tk, :].astype(F32).T.astype(vt_aug.dtype)
            for j in range(n_sub):
                vt_aug[kv, j, 0:hd, :] = v_t[j * hd:(j + 1) * hd, :]
                vt_aug[kv, j, hd:n_aug, :] = ones_row

    q2 = q_ref[0]
    lane = lax.broadcasted_iota(jnp.int32, (tq, bw), 1)
    qh = [jnp.where((lane >= j * hd) & (lane < (j + 1) * hd), q2, jnp.zeros((), q2.dtype))
          for j in range(n_sub)]
    fq = [fq_ref[0, j] for j in range(n_sub)]

    def step(kv, carry, masked, q_lo=0):
        off = pl.multiple_of(kv * tk, tk)
        k = k_ref[0, pl.ds(off, tk), :]
        nq = tq - q_lo
        out = []
        for j in range(n_sub):
            m_all, acc_all = carry[j]
            m, acc, fq_j = m_all[:, q_lo:], acc_all[:, q_lo:], fq[j][:, q_lo:]
            z = lax.dot_general(k, qh[j][q_lo:, :], (((1,), (1,)), ((), ())),
                                preferred_element_type=F32)
            z = z - fk_ref[0, j, pl.ds(off, tk), :]
            if masked:
                key = off + lax.broadcasted_iota(jnp.int32, (tk, tk), 0)
                qry = qi * tq + q_lo + lax.broadcasted_iota(jnp.int32, (tk, tk), 1)
                z_diag = jnp.where(key <= qry, z[:, :tk], NEG_INF)
                z = z_diag if nq == tk else jnp.concatenate([z_diag, z[:, tk:]], axis=1)
            m_new = jnp.maximum(m, jnp.max(z, axis=0, keepdims=True) + fq_j)
            a = jnp.exp(m - m_new)
            p = jnp.exp((z - (m_new - fq_j)).astype(BF16))
            acc = a * acc + jnp.dot(vt_aug[kv, j], p, preferred_element_type=F32)
            if q_lo:
                m_new = jnp.concatenate([m_all[:, :q_lo], m_new], axis=1)
                acc = jnp.concatenate([acc_all[:, :q_lo], acc], axis=1)
            out.append((m_new, acc))
        return tuple(out)

    carry = tuple((jnp.full((1, tq), NEG_INF, F32), jnp.zeros((n_aug, tq), F32))
                  for _ in range(n_sub))
    n_diag = tq // tk
    carry = lax.fori_loop(0, qi * n_diag, lambda kv, c: step(kv, c, False), carry)
    for i in range(n_diag):
        carry = step(qi * n_diag + i, carry, True, q_lo=i * tk)
    res = carry
    o_t = jnp.concatenate([acc[:hd, :] / acc[hd:hd + 1, :] for _, acc in res], axis=0)
    o_ref[0] = o_t.T.astype(o_ref.dtype)


def _attention(q, k, v, fq, fk, hd):
    bn, s, d = q.shape
    bw = LANES
    n_sub = bw // hd
    tk = min(s, ATTN_K_TILE)
    tq = min(s, ATTN_Q_TILE)
    return pl.pallas_call(
        functools.partial(_attn_kernel, tk=tk, hd=hd),
        grid=(bn, d // bw, s // tq),
        in_specs=[pl.BlockSpec((1, tq, bw), lambda b, h, i: (b, i, h)),
                  pl.BlockSpec((1, s, bw), lambda b, h, i: (b, 0, h)),
                  pl.BlockSpec((1, s, bw), lambda b, h, i: (b, 0, h)),
                  pl.BlockSpec((1, n_sub, 1, tq), lambda b, h, i: (b, h, 0, i)),
                  pl.BlockSpec((1, n_sub, s, 1), lambda b, h, i: (b, h, 0, 0))],
        out_specs=pl.BlockSpec((1, tq, bw), lambda b, h, i: (b, i, h)),
        out_shape=jax.ShapeDtypeStruct((bn, s, d), BF16),
        scratch_shapes=[pltpu.VMEM((s // tk, n_sub, hd + 2 * SUBLANES, tk), BF16)],
        compiler_params=_cparams(("arbitrary", "arbitrary", "arbitrary")),
        name="fox_attention",
    )(q, k, v, fq, fk)


def _attn_out_kernel(o_ref, x_ref, mod_ref, w_ref, b_ref, g_ref, bb_ref, out_ref, *, alpha):
    d = x_ref.shape[2]
    y = jnp.dot(o_ref[0], w_ref[...], preferred_element_type=F32) + b_ref[...]
    _, _, gate = _split_mod(mod_ref[0, 0], d)
    out_ref[0] = _layernorm(alpha * x_ref[0] + gate * y, g_ref[...], bb_ref[...])


def _attn_out(o, x, mod, layer, w_out, b_out, g, b, alpha):
    bn, s, d = x.shape
    t = min(s, TOKEN_TILE)
    row = lambda v: v.reshape(1, -1)
    const = lambda shape: pl.BlockSpec(shape, lambda i, j: (0,) * len(shape))
    return pl.pallas_call(
        functools.partial(_attn_out_kernel, alpha=alpha),
        grid=(bn, s // t),
        in_specs=[pl.BlockSpec((1, t, d), lambda i, j: (i, j, 0)),
                  pl.BlockSpec((1, t, d), lambda i, j: (i, j, 0)),
                  pl.BlockSpec((1, 1, 1, 3 * d), lambda i, j: (layer, i, 0, 0)),
                  const((d, d)), const((1, d)), const((1, d)), const((1, d))],
        out_specs=pl.BlockSpec((1, t, d), lambda i, j: (i, j, 0)),
        out_shape=jax.ShapeDtypeStruct((bn, s, d), F32),
        compiler_params=_cparams(("arbitrary", "arbitrary")),
        name="fox_out",
    )(o, x, mod, w_out.astype(BF16), row(b_out), row(g), row(b))


def _sort_desc(a):
    a = list(a)
    n = len(a)
    k = 2
    while k <= n:
        j = k // 2
        while j >= 1:
            for i in range(n):
                l = i ^ j
                if l > i:
                    hi, lo = jnp.maximum(a[i], a[l]), jnp.minimum(a[i], a[l])
                    a[i], a[l] = (hi, lo) if (i & k) == 0 else (lo, hi)
            j //= 2
        k *= 2
    return a


def _bitonic_merge_desc(t):
    t = list(t)
    n = len(t)
    j = n // 2
    while j >= 1:
        for i in range(n):
            l = i ^ j
            if l > i:
                t[i], t[l] = jnp.maximum(t[i], t[l]), jnp.minimum(t[i], t[l])
        j //= 2
    return t


def _merge_top(a, b):
    n = len(a)
    t = [a[i] if b[n - 1 - i] is None else jnp.maximum(a[i], b[n - 1 - i]) for i in range(n)]
    return _bitonic_merge_desc(t)


def _select_by_bits(table, bits):
    for b in bits:
        table = [jnp.where(b, table[2 * i + 1], table[2 * i]) for i in range(len(table) // 2)]
    return table[0]


def _rank_bits(v, s):
    n = len(v)
    known = []
    step = n // 2
    while step >= 1:
        pivots = [v[lo + step - 1] for lo in range(0, n, 2 * step)]
        known.append(_select_by_bits(pivots, known[::-1]) > s)
        step //= 2
    return known[::-1], v[n - 1] > s


def _top_values(s_ref, tk):
    n_keys = s_ref.shape[0]
    groups = [s_ref[g * SUBLANES:(g + 1) * SUBLANES, :] for g in range(n_keys // SUBLANES)]
    top = _sort_desc(groups[:tk])
    for g0 in range(tk, len(groups), tk):
        top = _merge_top(top, _sort_desc(groups[g0:g0 + tk]))
    shift = SUBLANES // 2
    while shift >= 1:
        top = _merge_top(top, [pltpu.roll(v, shift, 0) for v in top])
        shift //= 2
    return top


def _candidate_counts(v1, v2, tk):
    cand = {}
    for i in range(tk):
        for j in range(tk // (i + 1)):
            cand[i, j] = v1[i] + v2[j]
    half = tk // 2
    top = [cand[0, j] for j in range(tk)]
    lst_a = [cand[1, j] for j in range(half)] + [cand[i, 0] for i in range(tk - 1, half - 1, -1)]
    top = _merge_top(top, _bitonic_merge_desc(lst_a))
    rest = [cand[i, j] for i in range(2, half) for j in range(tk // (i + 1))]
    for g0 in range(0, len(rest), tk):
        grp = rest[g0:g0 + tk]
        if len(grp) == tk:
            grp = _sort_desc(grp)
        else:
            grp = _sort_desc(grp + [jnp.full_like(grp[0], NEG_INF)] * (tk - len(grp)))
        top = _merge_top(top, grp)
    tau = top[tk - 1]
    e1 = [jnp.exp(v1[i] - v1[0]) for i in range(tk)]
    e2 = [jnp.exp(v2[j] - v2[0]) for j in range(tk)]
    cnt, z = [], None
    for i in range(tk):
        c_i, z_i = None, None
        for j in range(tk // (i + 1)):
            sel = cand[i, j] >= tau
            one = jnp.where(sel, 1.0, 0.0)
            w = jnp.where(sel, e2[j], 0.0)
            c_i = one if c_i is None else c_i + one
            z_i = w if z_i is None else z_i + w
        cnt.append(c_i)
        z = e1[i] * z_i if z is None else z + e1[i] * z_i
    return cnt, 1.0 / z


def _route_kernel(x_ref, mod_ref, wq_ref, k1_ref, k2_ref,
                  ht_ref, c_ref, e1_ref, r2_ref, e2_ref,
                  q_scr, s_scr, v_scr, vh_scr, cs_scr, *, n_head, tk):
    t, d = x_ref.shape
    n_keys = k1_ref.shape[0]
    shift, scale, _ = _split_mod(mod_ref[0, 0], d)
    h = x_ref[...] * (1.0 + scale) + shift
    ht = h.T.astype(BF16)
    ht_ref[...] = ht
    q_scr[...] = jnp.dot(wq_ref[...], ht, preferred_element_type=F32).astype(BF16)

    def scores_and_top(hh, carry):
        for half, k_ref in enumerate((k1_ref, k2_ref)):
            off = pl.multiple_of((2 * hh + half) * n_keys, n_keys)
            s_scr[hh, half] = jnp.dot(k_ref[...], q_scr[pl.ds(off, n_keys), :],
                                      preferred_element_type=F32)
            top = _top_values(s_scr.at[hh, half], tk)
            for i in range(tk):
                v_scr[hh, half, i] = top[i]
                vh_scr[half, i, pl.ds(hh, 1), :] = top[i][0:1, :]
        return carry

    lax.fori_loop(0, n_head, scores_and_top, 0, unroll=True)

    v1 = [vh_scr[0, i] for i in range(tk)]
    v2 = [vh_scr[1, i] for i in range(tk)]
    cnt, inv_z = _candidate_counts(v1, v2, tk)
    for i in range(tk):
        cs_scr[i] = cnt[i]
    cs_scr[tk] = inv_z

    def dense(hh, carry):
        def row(idx):
            return jnp.broadcast_to(cs_scr[idx, pl.ds(hh, 1), :], (SUBLANES, t))
        cnt_h = [row(i) for i in range(tk)]
        inv = row(tk)
        v1h = [v_scr[hh, 0, i] for i in range(tk)]
        v2h = [v_scr[hh, 1, i] for i in range(tk)]
        for g in range(n_keys // SUBLANES):
            rows = slice(g * SUBLANES, (g + 1) * SUBLANES)
            s1 = s_scr[hh, 0, rows, :]
            s2 = s_scr[hh, 1, rows, :]
            bits1, below1 = _rank_bits(v1h, s1)
            bits2, below2 = _rank_bits(v2h, s2)
            c = jnp.where(below1, 0.0, _select_by_bits(cnt_h, bits1))
            r2 = jnp.where(below2, 1.0, 0.0)
            for b, weight in zip(bits2, (1.0, 2.0, 4.0, 8.0)):
                r2 = r2 + jnp.where(b, weight, 0.0)
            c_ref[hh, rows, :] = c
            e1_ref[hh, rows, :] = jnp.exp(s1 - v1h[0])
            r2_ref[hh, rows, :] = r2.astype(BF16)
            e2_ref[hh, rows, :] = (jnp.exp(s2 - v2h[0]) * inv).astype(BF16)
        return carry

    lax.fori_loop(0, n_head, dense, 0)


def _peer_route(x2, mod, layer, seq, wq_t, k1, k2, n_head):
    n_tok, d = x2.shape
    n_keys = k1.shape[0]
    t = min(n_tok, TOKEN_TILE)
    tk = PEER_TOPK
    const = lambda shape: pl.BlockSpec(shape, lambda i: (0,) * len(shape))
    tok3 = pl.BlockSpec((n_head, n_keys, t), lambda i: (0, 0, i))
    return pl.pallas_call(
        functools.partial(_route_kernel, n_head=n_head, tk=tk),
        grid=(n_tok // t,),
        in_specs=[pl.BlockSpec((t, d), lambda i: (i, 0)),
                  pl.BlockSpec((1, 1, 1, 3 * d), lambda i: (layer, (i * t) // seq, 0, 0)),
                  const(wq_t.shape), const(k1.shape), const(k2.shape)],
        out_specs=[pl.BlockSpec((d, t), lambda i: (0, i)), tok3, tok3, tok3, tok3],
        out_shape=[jax.ShapeDtypeStruct((d, n_tok), BF16),
                   jax.ShapeDtypeStruct((n_head, n_keys, n_tok), F32),
                   jax.ShapeDtypeStruct((n_head, n_keys, n_tok), F32),
                   jax.ShapeDtypeStruct((n_head, n_keys, n_tok), BF16),
                   jax.ShapeDtypeStruct((n_head, n_keys, n_tok), BF16)],
        scratch_shapes=[pltpu.VMEM((2 * n_head * n_keys, t), BF16),
                        pltpu.VMEM((n_head, 2, n_keys, t), F32),
                        pltpu.VMEM((n_head, 2, tk, SUBLANES, t), F32),
                        pltpu.VMEM((2, tk, SUBLANES, t), F32),
                        pltpu.VMEM((tk + 1, SUBLANES, t), F32)],
        compiler_params=_cparams(("arbitrary",)),
        name="peer_route",
    )(x2, mod, wq_t, k1, k2)


def _peer_dense_kernel(ht_ref, u_ref, un_ref, vt_ref, vp_ref, c_ref, e1_ref, r2_ref, e2_ref,
                       x_ref, mod_ref, g_ref, b_ref, o_ref, a0, a1, w0, w1, acc_scr,
                       *, n_head, n_keys, alpha):
    t, d = x_ref.shape
    chunk = a0.shape[0]
    n_c = u_ref.shape[0] // chunk
    n_a = chunk // n_keys
    assert n_c % 2 == 0
    pack = 2 * SUBLANES
    k = pl.program_id(1)
    a_buf, w_buf = (a0, a1), (w0, w1)

    def u_matmul(u_rows, dst):
        dst[...] = jnp.dot(u_rows, ht_ref[...], preferred_element_type=F32)

    def v_matmul(vt_chunk, w_scr):
        acc_scr[...] += jnp.dot(vt_chunk, w_scr[...], preferred_element_type=F32)

    @pl.when(k == 0)
    def _():
        w1[...] = jnp.zeros(w1.shape, w1.dtype)
        acc_scr[...] = jnp.zeros(acc_scr.shape, F32)
        u_matmul(u_ref[0:chunk, :], a0)

    def gate_pass(a_scr, w_scr, c):
        for a in range(n_a):
            key = (k * n_c + c) * n_a + a
            pre = a_scr[a * n_keys:(a + 1) * n_keys, :]
            act = (0.5 * pre * (1.0 + lax.erf(pre * INV_SQRT2))).astype(BF16)
            gate = [jnp.zeros((pack, t), BF16) for _ in range(n_keys // pack)]
            for hh in range(n_head):
                cnt = jnp.broadcast_to(c_ref[hh, pl.ds(key, 1), :], (pack, t)).astype(BF16)
                e1 = jnp.broadcast_to(e1_ref[hh, pl.ds(key, 1), :], (pack, t)).astype(BF16)
                for bc in range(n_keys // pack):
                    rows = slice(bc * pack, (bc + 1) * pack)
                    sel = jnp.where(r2_ref[hh, rows, :] < cnt, e2_ref[hh, rows, :],
                                    jnp.zeros((), BF16))
                    gate[bc] = gate[bc] + sel * e1
            for bc in range(n_keys // pack):
                r0 = a * n_keys + bc * pack
                w_scr[r0:r0 + pack, :] = act[bc * pack:(bc + 1) * pack, :] * gate[bc]

    for c in range(n_c):
        u_next = un_ref[...] if c == n_c - 1 else u_ref[(c + 1) * chunk:(c + 2) * chunk, :]
        u_matmul(u_next, a_buf[(c + 1) % 2])
        gate_pass(a_buf[c % 2], w_buf[c % 2], c)
        v_matmul(vp_ref[0] if c == 0 else vt_ref[c - 1], w_buf[(c - 1) % 2])

    @pl.when(k == pl.num_programs(1) - 1)
    def _():
        v_matmul(vt_ref[n_c - 1], w_buf[(n_c - 1) % 2])
        _, _, gate_mod = _split_mod(mod_ref[0, 0], d)
        y = acc_scr[...].T
        o_ref[...] = _layernorm(alpha * x_ref[...] + gate_mod * y, g_ref[...], b_ref[...])


def _peer_dense(x2, mod, layer, seq, ht, c, e1, r2, e2, u_bf, v_bf, g, b, alpha):
    n_tok, d = x2.shape
    n_head, n_keys, _ = c.shape
    n_exp = u_bf.shape[0]
    t = min(n_tok, TOKEN_TILE)
    chunk = PEER_STAGE_KEYS * n_keys
    n_c = PEER_STAGES_PER_STEP
    n_chunk = n_exp // chunk
    n_k = n_chunk // n_c
    vt3 = v_bf.reshape(n_chunk, chunk, d).transpose(0, 2, 1)
    tok3 = pl.BlockSpec((n_head, n_keys, t), lambda i, j: (0, 0, i))
    row = lambda v: v.reshape(1, -1)
    return pl.pallas_call(
        functools.partial(_peer_dense_kernel, n_head=n_head, n_keys=n_keys, alpha=alpha),
        grid=(n_tok // t, n_k),
        in_specs=[pl.BlockSpec((d, t), lambda i, j: (0, i)),
                  pl.BlockSpec((n_c * chunk, d), lambda i, j: (j, 0)),
                  pl.BlockSpec((chunk, d), lambda i, j: (jnp.minimum((j + 1) * n_c, n_chunk - 1), 0)),
                  pl.BlockSpec((n_c, d, chunk), lambda i, j: (j, 0, 0)),
                  pl.BlockSpec((1, d, chunk), lambda i, j: (jnp.maximum(j * n_c - 1, 0), 0, 0)),
                  tok3, tok3, tok3, tok3,
                  pl.BlockSpec((t, d), lambda i, j: (i, 0)),
                  pl.BlockSpec((1, 1, 1, 3 * d), lambda i, j: (layer, (i * t) // seq, 0, 0)),
                  pl.BlockSpec((1, d), lambda i, j: (0, 0)),
                  pl.BlockSpec((1, d), lambda i, j: (0, 0))],
        out_specs=pl.BlockSpec((t, d), lambda i, j: (i, 0)),
        out_shape=jax.ShapeDtypeStruct((n_tok, d), F32),
        scratch_shapes=[pltpu.VMEM((chunk, t), F32), pltpu.VMEM((chunk, t), F32),
                        pltpu.VMEM((chunk, t), BF16), pltpu.VMEM((chunk, t), BF16),
                        pltpu.VMEM((d, t), F32)],
        compiler_params=_cparams(("arbitrary", "arbitrary")),
        name="peer_dense",
    )(ht, u_bf, u_bf, vt3, vt3, c, e1, r2, e2, x2, mod, row(g), row(b))


def _peer_layer(x, mod, layer, wq, k1, k2, u, v, g, b, alpha):
    bn, s, d = x.shape
    n_keys, half_key = k1.shape
    n_head = wq.shape[1] // (2 * half_key)
    x2 = x.reshape(bn * s, d)
    ht, c, e1, r2, e2 = _peer_route(x2, mod, layer, s, wq.T.astype(BF16), k1.astype(BF16),
                                    k2.astype(BF16), n_head)
    out = _peer_dense(x2, mod, layer, s, ht, c, e1, r2, e2, u.astype(BF16), v.astype(BF16),
                      g, b, alpha)
    return out.reshape(bn, s, d)


def kernel(x, c, ada_mix_w, ada_mix_b, ln_mix_g, ln_mix_b, conv_in_w, conv_in_b, conv_dw_w, conv_dw_b, conv_ln_g, conv_ln_b, conv_out_w, conv_out_b, attn_in_w, attn_in_b, attn_out_w, attn_out_b, ada_ffn_w, ada_ffn_b, ln_ffn_g, ln_ffn_b, peer_query_w, peer_sub_keys_1, peer_sub_keys_2, peer_expert_u, peer_expert_v):
    depth = ada_mix_w.shape[0]
    d = x.shape[-1]
    n_attn_head = attn_in_w.shape[-1] - 3 * d
    alpha = (2 * depth) ** 0.25
    mod_mix = _adaln(c, ada_mix_w, ada_mix_b)
    mod_ffn = _adaln(c, ada_ffn_w, ada_ffn_b)
    for i in range(depth):
        j = i // 2
        if i % 2 == 0:
            x = _conv_mixer(x, mod_mix, i, conv_in_w[j], conv_in_b[j], conv_dw_w[j], conv_dw_b[j],
                            conv_ln_g[j], conv_ln_b[j], conv_out_w[j], conv_out_b[j],
                            ln_mix_g[i], ln_mix_b[i], alpha)
        else:
            q, k, v, cum = _qkv(x, mod_mix, i, attn_in_w[j], attn_in_b[j], n_attn_head)
            cum_t = jnp.swapaxes(cum[:, :, :n_attn_head], 1, 2)
            o = _attention(q, k, v, cum_t[:, :, None, :], cum_t[..., None], d // n_attn_head)
            x = _attn_out(o, x, mod_mix, i, attn_out_w[j], attn_out_b[j],
                          ln_mix_g[i], ln_mix_b[i], alpha)
        x = _peer_layer(x, mod_ffn, i, peer_query_w[i], peer_sub_keys_1[i], peer_sub_keys_2[i],
                        peer_expert_u[i], peer_expert_v[i], ln_ffn_g[i], ln_ffn_b[i], alpha)
    return x
```
